```python
import math
import jax, jax.numpy as jnp
from jax import lax
import numpy as np

D_MODEL = 1024
BATCH = 4
SEQ = 4096
DEPTH = 4

EXPAND = 2
D_INNER = EXPAND * D_MODEL
HEAD_DIM = 64
ROPE_THETA = 500000.0
ROPE_DIM = HEAD_DIM // 4
BLOCK = 128
EPS = 1e-6

DA_WIDTH = D_INNER // 2
DA_VDIM = 2 * HEAD_DIM
DA_HEADS = DA_WIDTH // DA_VDIM
HG_WIDTH = D_INNER - DA_WIDTH
HG_EXPAND = 128
HG_HEADS = HG_WIDTH // HG_EXPAND
HG_CHUNK = 64
SW_HEADS = D_INNER // HEAD_DIM
SW_KV_HEADS = SW_HEADS // 8
SW_WINDOW = 128

N_EVEN = (DEPTH + 1) // 2
N_ODD = DEPTH // 2
EVEN_SIZES = (DA_WIDTH, DA_WIDTH, DA_WIDTH, DA_WIDTH,
              HG_WIDTH, HG_WIDTH, HG_WIDTH, HG_WIDTH, HG_WIDTH)
ODD_SIZES = (D_INNER, SW_KV_HEADS * HEAD_DIM, SW_KV_HEADS * HEAD_DIM, D_INNER)
EVEN_IN = sum(EVEN_SIZES)
ODD_IN = sum(ODD_SIZES)

kernel_name = "hybrid_diffattn_hgrn2_swa_encoder"


def rmsnorm(x, w):
    xf = x.astype(jnp.float32)
    y = xf * lax.rsqrt(jnp.mean(xf * xf, axis=-1, keepdims=True) + EPS)
    return (y * w.astype(jnp.float32)).astype(x.dtype)


def split_cols(t, sizes):
    idx = [int(v) for v in np.cumsum(sizes)[:-1]]
    return jnp.split(t, idx, axis=-1)


def rope_tables(seq):
    inv = ROPE_THETA ** (-jnp.arange(0, ROPE_DIM, 2, dtype=jnp.float32) / ROPE_DIM)
    ang = jnp.arange(seq, dtype=jnp.float32)[:, None] * inv[None, :]
    return jnp.cos(ang), jnp.sin(ang)


def partial_rope(x, cos, sin):
    c = cos[None, :, None, :].astype(x.dtype)
    s = sin[None, :, None, :].astype(x.dtype)
    x1 = x[..., :ROPE_DIM // 2]
    x2 = x[..., ROPE_DIM // 2:ROPE_DIM]
    return jnp.concatenate([x1 * c - x2 * s, x2 * c + x1 * s, x[..., ROPE_DIM:]], axis=-1)


def differential_attention(q, k, v, lam_p, lam_init, subln_w, cos, sin):
    B, S, _ = q.shape
    H = DA_HEADS
    n = S // BLOCK
    q = partial_rope(q.reshape(B, S, 2 * H, HEAD_DIM), cos, sin).reshape(B, S, H, 2, HEAD_DIM)
    k = partial_rope(k.reshape(B, S, 2 * H, HEAD_DIM), cos, sin).reshape(B, S, H, 2, HEAD_DIM)
    v = v.reshape(B, S, H, DA_VDIM)
    lp = lam_p.astype(jnp.float32)
    lam = jnp.exp(jnp.sum(lp[0] * lp[1])) - jnp.exp(jnp.sum(lp[2] * lp[3])) + lam_init
    scale = HEAD_DIM ** -0.5
    qb = q.reshape(B, n, BLOCK, H, 2, HEAD_DIM).transpose(1, 0, 2, 3, 4, 5)

    def one_block(qx):
        s = jnp.einsum('bqhcd,bkhcd->bhcqk', qx, k, preferred_element_type=jnp.float32) * scale
        p = jax.nn.softmax(s, axis=-1)
        a = p[:, :, 0] - lam * p[:, :, 1]
        return jnp.einsum('bhqk,bkhv->bqhv', a.astype(v.dtype), v)

    o = lax.map(one_block, qb).transpose(1, 0, 2, 3, 4).reshape(B, S, H, DA_VDIM)
    o = rmsnorm(o, subln_w) * (1.0 - lam_init)
    return o.reshape(B, S, DA_WIDTH)


def hgrn2_chunk_scan(q, log_f, k, v):
    B, S, H, dk = q.shape
    dv = v.shape[-1]
    C = HG_CHUNK
    n = S // C

    def to_chunks(t):
        return t.reshape(B, n, C, H, t.shape[-1]).transpose(1, 0, 3, 2, 4)

    qc, gc, kc, vc = to_chunks(q), to_chunks(log_f), to_chunks(k), to_chunks(v)
    G = jnp.cumsum(gc, axis=3)
    tril = jnp.tril(jnp.ones((C, C), dtype=bool))[None, None, :, :, None]

    def step(state, xs):
        qb, Gb, kb, vb = xs
        diff = Gb[:, :, :, None, :] - Gb[:, :, None, :, :]
        decay = jnp.exp(jnp.where(tril, diff, -jnp.inf))
        A = jnp.einsum('bhtd,bhsd,bhtsd->bhts', qb, kb, decay)
        o = jnp.einsum('bhts,bhsv->bhtv', A, vb)
        o = o + jnp.einsum('bhtd,bhdv->bhtv', qb * jnp.exp(Gb), state)
        g_last = Gb[:, :, -1]
        k_dec = kb * jnp.exp(g_last[:, :, None, :] - Gb)
        new_state = jnp.exp(g_last)[..., None] * state + jnp.einsum('bhsd,bhsv->bhdv', k_dec, vb)
        return new_state, o

    s0 = jnp.zeros((B, H, dk, dv), jnp.float32)
    _, o = lax.scan(step, s0, (qc, G, kc, vc))
    return o.transpose(1, 0, 3, 2, 4).reshape(B, S, H, dv)


def hgrn2_bidirectional(q, f_fwd, f_bwd, i, lb, norm_w):
    B, S, _ = q.shape
    shp = (B, S, HG_HEADS, HG_EXPAND)
    qf = jax.nn.silu(q.astype(jnp.float32)).reshape(shp)
    vf = i.astype(jnp.float32).reshape(shp)

    def gates(z, lb_d):
        z = z.astype(jnp.float32).reshape(shp)
        lb_d = lb_d.reshape(HG_HEADS, HG_EXPAND)
        log_f = jnp.logaddexp(jnp.log(lb_d), jnp.log1p(-lb_d) + jax.nn.log_sigmoid(z))
        key = (1.0 - lb_d) * jax.nn.sigmoid(-z)
        return log_f, key

    lf_f, k_f = gates(f_fwd, lb[0])
    o_f = hgrn2_chunk_scan(qf, lf_f, k_f, vf)
    lf_b, k_b = gates(f_bwd, lb[1])
    rev = lambda t: jnp.flip(t, axis=1)
    o_b = rev(hgrn2_chunk_scan(rev(qf), rev(lf_b), rev(k_b), rev(vf)))
    o = rmsnorm(o_f + o_b, norm_w)
    return o.reshape(B, S, HG_WIDTH).astype(q.dtype)


def window_gqa_with_sink(q, k, v, sink, cos, sin):
    B, S, _ = q.shape
    Hkv = SW_KV_HEADS
    G = SW_HEADS // Hkv
    n = S // BLOCK
    q = partial_rope(q.reshape(B, S, SW_HEADS, HEAD_DIM), cos, sin)
    k = partial_rope(k.reshape(B, S, Hkv, HEAD_DIM), cos, sin)
    v = v.reshape(B, S, Hkv, HEAD_DIM)

    def banded(t):
        tp = jnp.pad(t, ((0, 0), (BLOCK, BLOCK), (0, 0), (0, 0))).reshape(B, n + 2, BLOCK, Hkv, HEAD_DIM)
        band = jnp.concatenate([tp[:, :-2], tp[:, 1:-1], tp[:, 2:]], axis=2)
        return band.transpose(1, 0, 2, 3, 4)

    qb = q.reshape(B, n, BLOCK, Hkv, G, HEAD_DIM).transpose(1, 0, 2, 3, 4, 5)
    kb, vb = banded(k), banded(v)
    sink_g = sink.astype(jnp.float32).reshape(Hkv, G)[None, :, :, None, None]
    q_idx = jnp.arange(BLOCK)[:, None]
    k_off = jnp.arange(3 * BLOCK)[None, :] - BLOCK
    in_window = jnp.abs(k_off - q_idx) <= SW_WINDOW
    scale = HEAD_DIM ** -0.5

    def one_block(args):
        blk, qx, kx, vx = args
        s = jnp.einsum('bqhgd,bkhd->bhgqk', qx, kx, preferred_element_type=jnp.float32) * scale
        key_pos = blk * BLOCK + k_off
        valid = in_window & (key_pos >= 0) & (key_pos < S)
        s = jnp.where(valid, s, -jnp.inf)
        m = jnp.maximum(jnp.max(s, axis=-1, keepdims=True), sink_g)
        p = jnp.exp(s - m)
        p = p / (jnp.sum(p, axis=-1, keepdims=True) + jnp.exp(sink_g - m))
        return jnp.einsum('bhgqk,bkhd->bqhgd', p.astype(vx.dtype), vx)

    o = lax.map(one_block, (jnp.arange(n), qb, kb, vb))
    return o.transpose(1, 0, 2, 3, 4, 5).reshape(B, S, SW_HEADS * HEAD_DIM)


def even_layer(x, norm_w, w_in, w_out, lam_p, lam_init, subln_w, hg_norm_w, lb, cos, sin):
    h = rmsnorm(x, norm_w)
    proj = jnp.einsum('bsd,de->bse', h, w_in)
    qa, ka, va, ga, qh, ffw, fbw, ih, gh = split_cols(proj, EVEN_SIZES)
    oa = differential_attention(qa, ka, va, lam_p, lam_init, subln_w, cos, sin) * jax.nn.silu(ga)
    ob = hgrn2_bidirectional(qh, ffw, fbw, ih, lb, hg_norm_w) * jax.nn.silu(gh)
    return jnp.einsum('bse,ed->bsd', jnp.concatenate([oa, ob], axis=-1), w_out)


def odd_layer(x, norm_w, w_in, w_out, sink, cos, sin):
    h = rmsnorm(x, norm_w)
    proj = jnp.einsum('bsd,de->bse', h, w_in)
    qc, kc, vc, gc = split_cols(proj, ODD_SIZES)
    oc = window_gqa_with_sink(qc, kc, vc, sink, cos, sin) * jax.nn.silu(gc)
    return jnp.einsum('bse,ed->bsd', oc, w_out)


def setup_inputs(seed: int = 0) -> dict:
    key = jax.random.key(seed)
    ks = jax.random.split(key, 16)
    f32 = jnp.float32
    nrm = lambda k, shp, sc: jax.random.normal(k, shp, f32) * sc
    return {
        "x": nrm(ks[0], (BATCH, SEQ, D_MODEL), 1.0),
        "norm_even": 1.0 + nrm(ks[1], (N_EVEN, D_MODEL), 0.02),
        "w_in_even": nrm(ks[2], (N_EVEN, D_MODEL, EVEN_IN), D_MODEL ** -0.5),
        "lambda_qk": nrm(ks[3], (N_EVEN, 4, HEAD_DIM), 0.1),
        "subln_w": 1.0 + nrm(ks[4], (N_EVEN, DA_VDIM), 0.02),
        "hgrn_norm_w": 1.0 + nrm(ks[5], (N_EVEN, HG_EXPAND), 0.02),
        "hgrn_lb_logits": nrm(ks[6], (2, N_EVEN, HG_WIDTH), 0.5),
        "w_out_even": nrm(ks[7], (N_EVEN, D_INNER, D_MODEL), D_INNER ** -0.5),
        "norm_odd": 1.0 + nrm(ks[8], (N_ODD, D_MODEL), 0.02),
        "w_in_odd": nrm(ks[9], (N_ODD, D_MODEL, ODD_IN), D_MODEL ** -0.5),
        "sink_logits": nrm(ks[10], (N_ODD, SW_HEADS), 0.5),
        "w_out_odd": nrm(ks[11], (N_ODD, D_INNER, D_MODEL), D_INNER ** -0.5),
        "final_norm": 1.0 + nrm(ks[12], (D_MODEL,), 0.02),
    }


def reference(x, norm_even, w_in_even, lambda_qk, subln_w, hgrn_norm_w, hgrn_lb_logits,
              w_out_even, norm_odd, w_in_odd, sink_logits, w_out_odd, final_norm):
    S = x.shape[1]
    cos, sin = rope_tables(S)
    lb_cum = jnp.cumsum(jax.nn.softmax(hgrn_lb_logits.astype(jnp.float32), axis=1), axis=1)
    lb_all = lb_cum - lb_cum[:, :1]
    for layer in range(DEPTH):
        j = layer // 2
        if layer % 2 == 0:
            lam_init = 0.8 - 0.6 * math.exp(-0.3 * layer)
            x = x + even_layer(x, norm_even[j], w_in_even[j], w_out_even[j], lambda_qk[j],
                               lam_init, subln_w[j], hgrn_norm_w[j], lb_all[:, j], cos, sin)
        else:
            x = x + odd_layer(x, norm_odd[j], w_in_odd[j], w_out_odd[j], sink_logits[j], cos, sin)
    return rmsnorm(x, final_norm)
```

```python
import functools
import math

import jax
import jax.numpy as jnp
import numpy as np
from jax import lax
from jax.experimental import pallas as pl
from jax.experimental.pallas import tpu as pltpu

D_MODEL = 1024
BATCH = 4
SEQ = 4096
DEPTH = 4
D_INNER = 2 * D_MODEL
HEAD_DIM = 64
ROPE_THETA = 500000.0
ROPE_DIM = HEAD_DIM // 4
EPS = 1e-6
DA_WIDTH = D_INNER // 2
DA_VDIM = 2 * HEAD_DIM
DA_HEADS = DA_WIDTH // DA_VDIM
HG_WIDTH = D_INNER - DA_WIDTH
HG_EXPAND = 128
HG_HEADS = HG_WIDTH // HG_EXPAND
SW_HEADS = D_INNER // HEAD_DIM
SW_KV_HEADS = SW_HEADS // 8
SW_GROUP = SW_HEADS // SW_KV_HEADS
SW_WINDOW = 128
SW_BLOCK = 128

M_TOK = BATCH * SEQ
LANES = 128
F32 = jnp.float32
BF16 = jnp.bfloat16
NEG_BIG = -1e30

VMEM_LIMIT = 48 * 1024 * 1024

NORM_TM = 512
PROJ_TM = 1024
PROJ_TN = 512
OUT_TM = 512
DA_TQ = 512
DA_TK = 512
HG_CHUNK = 128

_NT = (((1,), (1,)), ((), ()))
_TN = (((0,), (0,)), ((), ()))


def _cparams(sem):
    return pltpu.CompilerParams(dimension_semantics=sem, vmem_limit_bytes=VMEM_LIMIT)


def _rms(x, w):
    return x * lax.rsqrt(jnp.mean(x * x, axis=-1, keepdims=True) + EPS) * w


def _norm_kernel(x_ref, w_ref, h_ref):
    h_ref[...] = _rms(x_ref[...], w_ref[...]).astype(h_ref.dtype)


def _norm(x, w):
    return pl.pallas_call(
        _norm_kernel,
        grid=(M_TOK // NORM_TM,),
        in_specs=[pl.BlockSpec((NORM_TM, D_MODEL), lambda i: (i, 0)),
                  pl.BlockSpec((1, D_MODEL), lambda i: (0, 0))],
        out_specs=pl.BlockSpec((NORM_TM, D_MODEL), lambda i: (i, 0)),
        out_shape=jax.ShapeDtypeStruct((M_TOK, D_MODEL), BF16),
        compiler_params=_cparams(("parallel",)),
        name="rmsnorm0",
    )(x, w.reshape(1, D_MODEL))


def _rope_cols(acc, rope_ref, scale):
    c, s1, s2 = rope_ref[0], rope_ref[1], rope_ref[2]
    outs = []
    for t in range(acc.shape[1] // LANES):
        xs = acc[:, t * LANES:(t + 1) * LANES]
        y = xs * c + pltpu.roll(xs, LANES - ROPE_DIM // 2, 1) * s1 + pltpu.roll(xs, ROPE_DIM // 2, 1) * s2
        outs.append(y * scale if scale != 1.0 else y)
    return jnp.concatenate(outs, axis=1)


def _silu(x):
    return x * (1.0 / (1.0 + jnp.exp(-x)))


def _proj_kernel(h_ref, w_ref, rope_ref, o_ref, *, groups):
    j = pl.program_id(1)
    acc = jnp.dot(h_ref[...], w_ref[...], preferred_element_type=F32)
    for lo, hi, mode in groups:
        @pl.when((j >= lo) & (j < hi))
        def _(mode=mode):
            if mode == "rope_q":
                y = _rope_cols(acc, rope_ref, HEAD_DIM ** -0.5)
            elif mode == "rope_k":
                y = _rope_cols(acc, rope_ref, 1.0)
            elif mode == "silu":
                y = _silu(acc)
            else:
                y = acc
            o_ref[...] = y.astype(o_ref.dtype)


def _proj(h, w, rope_tab, groups, n_out, out_dtype, col_map, name):
    n_j = n_out // PROJ_TN
    s_blocks = SEQ // PROJ_TM
    return pl.pallas_call(
        functools.partial(_proj_kernel, groups=groups),
        grid=(M_TOK // PROJ_TM, n_j),
        in_specs=[pl.BlockSpec((PROJ_TM, D_MODEL), lambda i, j: (i, 0)),
                  pl.BlockSpec((D_MODEL, PROJ_TN), lambda i, j: (0, col_map(j))),
                  pl.BlockSpec((3, PROJ_TM, LANES), lambda i, j: (0, i % s_blocks, 0))],
        out_specs=pl.BlockSpec((PROJ_TM, PROJ_TN), lambda i, j: (i, j)),
        out_shape=jax.ShapeDtypeStruct((M_TOK, n_out), out_dtype),
        compiler_params=_cparams(("parallel", "arbitrary")),
        name=name,
    )(h, w, rope_tab)


def _outproj_kernel(*refs, n_in, final):
    o_refs = refs[:n_in]
    w_ref, x_ref, nw_ref = refs[n_in:n_in + 3]
    outs = refs[n_in + 3:]
    acc = x_ref[...]
    k0 = 0
    for o_ref in o_refs:
        kw = o_ref.shape[1]
        acc = acc + jnp.dot(o_ref[...], w_ref[k0:k0 + kw, :], preferred_element_type=F32)
        k0 += kw
    y = _rms(acc, nw_ref[...])
    if final:
        outs[0][...] = y
    else:
        outs[0][...] = acc
        outs[1][...] = y.astype(BF16)


def _outproj(o_list, w, x, next_norm_w, final, name):
    n_in = len(o_list)
    row = lambda i: (i, 0)
    in_specs = [pl.BlockSpec((OUT_TM, o.shape[1]), row) for o in o_list]
    in_specs += [pl.BlockSpec((D_INNER, D_MODEL), lambda i: (0, 0)),
                 pl.BlockSpec((OUT_TM, D_MODEL), row),
                 pl.BlockSpec((1, D_MODEL), lambda i: (0, 0))]
    if final:
        out_shape = [jax.ShapeDtypeStruct((M_TOK, D_MODEL), F32)]
        out_specs = [pl.BlockSpec((OUT_TM, D_MODEL), row)]
    else:
        out_shape = [jax.ShapeDtypeStruct((M_TOK, D_MODEL), F32),
                     jax.ShapeDtypeStruct((M_TOK, D_MODEL), BF16)]
        out_specs = [pl.BlockSpec((OUT_TM, D_MODEL), row), pl.BlockSpec((OUT_TM, D_MODEL), row)]
    return pl.pallas_call(
        functools.partial(_outproj_kernel, n_in=n_in, final=final),
        grid=(M_TOK // OUT_TM,),
        in_specs=in_specs, out_specs=out_specs, out_shape=out_shape,
        compiler_params=_cparams(("parallel",)),
        name=name,
    )(*o_list, w, x, next_norm_w.reshape(1, D_MODEL))


def _dattn_kernel(lam_ref, sw_ref, q_ref, k_ref, v_ref, g_ref, o_ref, *, lam_init):
    lp = lam_ref[...]
    lam = (jnp.exp(jnp.sum(lp[0:1] * lp[1:2], axis=1, keepdims=True))
           - jnp.exp(jnp.sum(lp[2:3] * lp[3:4], axis=1, keepdims=True)) + lam_init)
    q = q_ref[...]
    lane = lax.broadcasted_iota(jnp.int32, q.shape, 1)
    q_maps = (jnp.where(lane < HEAD_DIM, q, jnp.zeros_like(q)),
              jnp.where(lane >= HEAD_DIM, q, jnp.zeros_like(q)))
    tq = q.shape[0]

    def body(c, carry):
        r0 = pl.multiple_of(c * DA_TK, DA_TK)
        kb = k_ref[pl.ds(r0, DA_TK), :]
        vb = v_ref[pl.ds(r0, DA_TK), :]
        new = []
        for mp in range(2):
            m, l, a = carry[mp]
            s = lax.dot_general(q_maps[mp], kb, _NT, preferred_element_type=F32)
            mn = jnp.maximum(m, jnp.max(s, axis=1, keepdims=True))
            alpha = jnp.exp(m - mn)
            p = jnp.exp(s - mn)
            l = alpha * l + jnp.sum(p, axis=1, keepdims=True)
            a = alpha * a + jnp.dot(p.astype(BF16), vb, preferred_element_type=F32)
            new.append((mn, l, a))
        return tuple(new)

    init = tuple((jnp.full((tq, 1), NEG_BIG, F32), jnp.zeros((tq, 1), F32), jnp.zeros((tq, DA_VDIM), F32))
                 for _ in range(2))
    (_, l1, a1), (_, l2, a2) = lax.fori_loop(0, SEQ // DA_TK, body, init)
    o = a1 / l1 - lam * (a2 / l2)
    y = _rms(o, sw_ref[...]) * (1.0 - lam_init)
    o_ref[...] = (y * g_ref[...].astype(F32)).astype(o_ref.dtype)


def _dattn(pm, lam_p, subln_w, lam_init, name):
    nq = SEQ // DA_TQ
    hb = DA_WIDTH // LANES
    return pl.pallas_call(
        functools.partial(_dattn_kernel, lam_init=lam_init),
        grid=(BATCH, DA_HEADS, nq),
        in_specs=[pl.BlockSpec((4, HEAD_DIM), lambda b, h, i: (0, 0)),
                  pl.BlockSpec((1, DA_VDIM), lambda b, h, i: (0, 0)),
                  pl.BlockSpec((DA_TQ, LANES), lambda b, h, i: (b * nq + i, h)),
                  pl.BlockSpec((SEQ, LANES), lambda b, h, i: (b, hb + h)),
                  pl.BlockSpec((SEQ, LANES), lambda b, h, i: (b, 2 * hb + h)),
                  pl.BlockSpec((DA_TQ, LANES), lambda b, h, i: (b * nq + i, 3 * hb + h))],
        out_specs=pl.BlockSpec((DA_TQ, LANES), lambda b, h, i: (b * nq + i, h)),
        out_shape=jax.ShapeDtypeStruct((M_TOK, DA_WIDTH), BF16),
        compiler_params=_cparams(("parallel", "parallel", "arbitrary")),
        name=name,
    )(lam_p, subln_w.reshape(1, DA_VDIM), pm, pm, pm, pm)


def _bcast_row(p, n, off):
    c = p.shape[0]
    if n >= 16:
        pieces = [jnp.broadcast_to(p[b * n + off:b * n + off + 1, :], (n, p.shape[1])) for b in range(c // n)]
        return jnp.concatenate(pieces, axis=0) if len(pieces) > 1 else pieces[0]
    row = lax.broadcasted_iota(jnp.int32, p.shape, 0)
    d = (row % n) - off
    out = p
    for delta in range(-off, n - off):
        if delta == 0:
            continue
        out = jnp.where(d == delta, pltpu.roll(p, delta % c, 0), out)
    return out


def _hgrn_gates(z, log_lb, log1m_lb, one_m_lb):
    e = jnp.exp(-jnp.abs(z))
    d = 1.0 + e
    r = 1.0 / d
    log_sig = jnp.minimum(z, 0.0) - jnp.log(d)
    b = log1m_lb + log_sig
    log_f = jnp.maximum(log_lb, b) + jnp.log(1.0 + jnp.exp(-jnp.abs(log_lb - b)))
    key = one_m_lb * jnp.where(z >= 0.0, e * r, r)
    return log_f, key


def _hgrn_kernel(q_ref, v_ref, g_ref, zf_ref, zb_ref, lb_ref, nw_ref, o_ref,
                 acc_ref, qgb_ref, kdb_ref, decb_ref, sf_ref, sb_ref):
    C = HG_CHUNK
    n_chunks = SEQ // C
    lb = lb_ref[...]
    log_lb = jnp.log(lb)
    log1m_lb = jnp.log(1.0 - lb)
    one_m_lb = 1.0 - lb
    row = lax.broadcasted_iota(jnp.int32, (C, HG_EXPAND), 0)
    arow = lax.broadcasted_iota(jnp.int32, (C, C), 0)
    acol = lax.broadcasted_iota(jnp.int32, (C, C), 1)

    sf_ref[...] = jnp.zeros_like(sf_ref)
    sb_ref[...] = jnp.zeros_like(sb_ref)

    def fwd_chunk(n, _):
        r0 = pl.multiple_of(n * C, C)
        q = q_ref[pl.ds(r0, C), :].astype(F32)
        v = v_ref[pl.ds(r0, C), :]
        pf, kf = _hgrn_gates(zf_ref[pl.ds(r0, C), :], log_lb[0:1], log1m_lb[0:1], one_m_lb[0:1])
        pb, kb = _hgrn_gates(zb_ref[pl.ds(r0, C), :], log_lb[1:2], log1m_lb[1:2], one_m_lb[1:2])
        diag = jnp.sum(q * (kf + kb), axis=1, keepdims=True)
        a = jnp.where(arow == acol, diag, 0.0)
        zero = jnp.zeros_like(q)
        blk = 2
        while blk <= C:
            half = blk // 2
            right = (row % blk) >= half
            bf = _bcast_row(pf, blk, half - 1)
            bb = _bcast_row(pb, blk, half)
            ef = jnp.exp(jnp.where(right, pf, bf - pf))
            eb = jnp.exp(jnp.where(right, bb - pb, pb))
            qn = jnp.concatenate([q * jnp.where(right, ef, zero), q * jnp.where(right, zero, eb)], axis=1)
            kn = jnp.concatenate([kf * jnp.where(right, zero, ef), kb * jnp.where(right, eb, zero)], axis=1)
            an = lax.dot_general(qn.astype(BF16), kn.astype(BF16), _NT, preferred_element_type=F32)
            a = a + jnp.where((arow // blk) == (acol // blk), an, 0.0)
            pf = pf + jnp.where(right, bf, zero)
            pb = pb + jnp.where(right, zero, bb)
            blk *= 2
        o = jnp.dot(a.astype(BF16), v, preferred_element_type=F32)
        pf_last = pf[C - 1:C, :]
        st = sf_ref[...]
        o = o + lax.dot_general((q * jnp.exp(pf)).astype(BF16), st.astype(BF16), _NT,
                                preferred_element_type=F32)
        kdec = (kf * jnp.exp(pf_last - pf)).astype(BF16)
        sf_ref[...] = jnp.exp(pf_last) * st + lax.dot_general(v, kdec, _TN, preferred_element_type=F32)
        acc_ref[pl.ds(r0, C), :] = o
        pb_first = pb[0:1, :]
        qgb_ref[pl.ds(r0, C), :] = (q * jnp.exp(pb)).astype(BF16)
        kdb_ref[pl.ds(r0, C), :] = (kb * jnp.exp(pb_first - pb)).astype(BF16)
        decb_ref[pl.ds(n, 1), :] = jnp.exp(pb_first)
        return 0

    lax.fori_loop(0, n_chunks, fwd_chunk, 0)

    def bwd_chunk(it, _):
        n = n_chunks - 1 - it
        r0 = pl.multiple_of(n * C, C)
        v = v_ref[pl.ds(r0, C), :]
        st = sb_ref[...]
        o = acc_ref[pl.ds(r0, C), :] + lax.dot_general(qgb_ref[pl.ds(r0, C), :], st.astype(BF16), _NT,
                                                       preferred_element_type=F32)
        sb_ref[...] = decb_ref[pl.ds(n, 1), :] * st + lax.dot_general(
            v, kdb_ref[pl.ds(r0, C), :], _TN, preferred_element_type=F32)
        y = _rms(o, nw_ref[...]) * g_ref[pl.ds(r0, C), :].astype(F32)
        o_ref[pl.ds(r0, C), :] = y.astype(o_ref.dtype)
        return 0

    lax.fori_loop(0, n_chunks, bwd_chunk, 0)


def _hgrn(pm, pg, lb, norm_w, name):
    hb = HG_WIDTH // LANES
    seq_blk = lambda off: pl.BlockSpec((SEQ, LANES), lambda b, h: (b, off + h))
    return pl.pallas_call(
        _hgrn_kernel,
        grid=(BATCH, HG_HEADS),
        in_specs=[seq_blk(4 * hb), seq_blk(5 * hb), seq_blk(6 * hb), seq_blk(0), seq_blk(hb),
                  pl.BlockSpec((2, LANES), lambda b, h: (0, h)),
                  pl.BlockSpec((1, HG_EXPAND), lambda b, h: (0, 0))],
        out_specs=pl.BlockSpec((SEQ, LANES), lambda b, h: (b, h)),
        out_shape=jax.ShapeDtypeStruct((M_TOK, HG_WIDTH), BF16),
        scratch_shapes=[pltpu.VMEM((SEQ, HG_EXPAND), F32),
                        pltpu.VMEM((SEQ, HG_EXPAND), BF16),
                        pltpu.VMEM((SEQ, HG_EXPAND), BF16),
                        pltpu.VMEM((SEQ // HG_CHUNK, HG_EXPAND), F32),
                        pltpu.VMEM((HG_EXPAND, HG_EXPAND), F32),
                        pltpu.VMEM((HG_EXPAND, HG_EXPAND), F32)],
        compiler_params=_cparams(("parallel", "parallel")),
        name=name,
    )(pm, pm, pm, pg, pg, lb, norm_w.reshape(1, HG_EXPAND))


def _wattn_kernel(sink_ref, q_ref, kp_ref, kc_ref, kn_ref, vp_ref, vc_ref, vn_ref, g_ref, o_ref):
    i = pl.program_id(1)
    rows = SW_GROUP * SW_BLOCK
    qi = lax.broadcasted_iota(jnp.int32, (rows, 3 * SW_BLOCK), 0) % SW_BLOCK
    kj = lax.broadcasted_iota(jnp.int32, (rows, 3 * SW_BLOCK), 1) - SW_BLOCK
    kpos = i * SW_BLOCK + kj
    valid = (jnp.abs(kj - qi) <= SW_WINDOW) & (kpos >= 0) & (kpos < SEQ)
    lane = lax.broadcasted_iota(jnp.int32, (SW_BLOCK, LANES), 1)
    low = lane < HEAD_DIM
    for c in range(SW_KV_HEADS):
        cs = slice(c * LANES, (c + 1) * LANES)
        k2 = jnp.concatenate([kp_ref[:, cs], kc_ref[:, cs], kn_ref[:, cs]], axis=0)
        v2 = jnp.concatenate([vp_ref[:, cs], vc_ref[:, cs], vn_ref[:, cs]], axis=0)
        parts, sinks = [], []
        for p in range(SW_GROUP // 2):
            col = c * SW_GROUP * HEAD_DIM + p * LANES
            t = q_ref[:, col:col + LANES]
            parts += [jnp.where(low, t, jnp.zeros_like(t)), jnp.where(low, jnp.zeros_like(t), t)]
        for r in range(SW_GROUP):
            sinks.append(jnp.full((SW_BLOCK, 1), sink_ref[c * SW_GROUP + r], F32))
        lhs = jnp.concatenate(parts, axis=0)
        sink = jnp.concatenate(sinks, axis=0)
        s = lax.dot_general(lhs, k2, _NT, preferred_element_type=F32)
        s = jnp.where(valid, s, -jnp.inf)
        m = jnp.maximum(jnp.max(s, axis=1, keepdims=True), sink)
        p_ = jnp.exp(s - m)
        den = jnp.sum(p_, axis=1, keepdims=True) + jnp.exp(sink - m)
        o = jnp.dot(p_.astype(BF16), v2, preferred_element_type=F32) / den
        for p in range(SW_GROUP // 2):
            col = c * SW_GROUP * HEAD_DIM + p * LANES
            pair = jnp.where(low, o[2 * p * SW_BLOCK:(2 * p + 1) * SW_BLOCK],
                             o[(2 * p + 1) * SW_BLOCK:(2 * p + 2) * SW_BLOCK])
            o_ref[:, col:col + LANES] = (pair * g_ref[:, col:col + LANES].astype(F32)).astype(o_ref.dtype)


def _wattn(po, sink, name):
    nb = SEQ // SW_BLOCK
    kv_w = SW_KV_HEADS * LANES
    kcol = 2 * D_INNER // kv_w
    vcol = kcol + 1

    def kv(col, shift):
        return pl.BlockSpec((SW_BLOCK, kv_w), lambda b, i: (b * nb + jnp.clip(i + shift, 0, nb - 1), col))

    return pl.pallas_call(
        _wattn_kernel,
        grid=(BATCH, nb),
        in_specs=[pl.BlockSpec(memory_space=pltpu.SMEM),
                  pl.BlockSpec((SW_BLOCK, D_INNER), lambda b, i: (b * nb + i, 0)),
                  kv(kcol, -1), kv(kcol, 0), kv(kcol, 1),
                  kv(vcol, -1), kv(vcol, 0), kv(vcol, 1),
                  pl.BlockSpec((SW_BLOCK, D_INNER), lambda b, i: (b * nb + i, 1))],
        out_specs=pl.BlockSpec((SW_BLOCK, D_INNER), lambda b, i: (b * nb + i, 0)),
        out_shape=jax.ShapeDtypeStruct((M_TOK, D_INNER), BF16),
        compiler_params=_cparams(("parallel", "arbitrary")),
        name=name,
    )(sink, po, po, po, po, po, po, po, po)


def _rope_tables():
    inv = ROPE_THETA ** (-jnp.arange(0, ROPE_DIM, 2, dtype=F32) / ROPE_DIM)
    ang = jnp.arange(SEQ, dtype=F32)[:, None] * inv[None, :]
    cos, sin = jnp.cos(ang), jnp.sin(ang)
    half = ROPE_DIM // 2
    m = np.arange(LANES) % HEAD_DIM
    idx = jnp.asarray(m % half)
    cos_l, sin_l = cos[:, idx], sin[:, idx]
    c = jnp.where(jnp.asarray(m < ROPE_DIM)[None, :], cos_l, 1.0)
    s1 = jnp.where(jnp.asarray(m < half)[None, :], -sin_l, 0.0)
    s2 = jnp.where(jnp.asarray((m >= half) & (m < ROPE_DIM))[None, :], sin_l, 0.0)
    return jnp.stack([c, s1, s2]).astype(F32)


def _dup_heads(w):
    d, n = w.shape[0], w.shape[1] // HEAD_DIM
    return jnp.repeat(w.reshape(d, n, 1, HEAD_DIM), 2, axis=2).reshape(d, 2 * n * HEAD_DIM)


def kernel(x, norm_even, w_in_even, lambda_qk, subln_w, hgrn_norm_w, hgrn_lb_logits, w_out_even,
           norm_odd, w_in_odd, sink_logits, w_out_odd, final_norm):
    rope_tab = _rope_tables()
    lb_cum = jnp.cumsum(jax.nn.softmax(hgrn_lb_logits.astype(F32), axis=1), axis=1)
    lb_all = lb_cum - lb_cum[:, :1]

    n_tiles = lambda width: width // PROJ_TN
    t1 = n_tiles(DA_WIDTH)
    even_groups = ((0, t1, "rope_q"), (t1, 2 * t1, "rope_k"), (2 * t1, 3 * t1, "plain"),
                   (3 * t1, 5 * t1, "silu"), (5 * t1, 6 * t1, "plain"), (6 * t1, 7 * t1, "silu"))
    n_gate_tiles = 2 * t1
    even_map = lambda j: jnp.where(j < 5 * t1, j, j + n_gate_tiles)
    gate_groups = ((0, n_gate_tiles, "plain"),)
    gate_map = lambda j: j + 5 * t1
    tq, tkv = n_tiles(D_INNER), n_tiles(2 * SW_KV_HEADS * HEAD_DIM)
    odd_groups = ((0, tq, "rope_q"), (tq, 2 * tq, "silu"), (2 * tq, 2 * tq + tkv, "rope_k"),
                  (2 * tq + tkv, 2 * tq + 2 * tkv, "plain"))

    xf = x.reshape(M_TOK, D_MODEL)
    h = _norm(xf, norm_even[0])
    out = None
    for layer in range(DEPTH):
        j = layer // 2
        last = layer == DEPTH - 1
        if last:
            next_norm = final_norm
        elif layer % 2 == 0:
            next_norm = norm_odd[j]
        else:
            next_norm = norm_even[j + 1]
        if layer % 2 == 0:
            lam_init = 0.8 - 0.6 * math.exp(-0.3 * layer)
            w = w_in_even[j].astype(BF16)
            pm = _proj(h, w, rope_tab, even_groups, 7 * DA_WIDTH, BF16, even_map, f"proj_even{j}")
            pg = _proj(h, w, rope_tab, gate_groups, 2 * HG_WIDTH, F32, gate_map, f"proj_gate{j}")
            oa = _dattn(pm, lambda_qk[j], subln_w[j], lam_init, f"diff_attn{j}")
            ob = _hgrn(pm, pg, lb_all[:, j], hgrn_norm_w[j], f"hgrn{j}")
            res = _outproj([oa, ob], w_out_even[j].astype(BF16), xf, next_norm, last, f"outproj_even{j}")
        else:
            wi = w_in_odd[j]
            kv_w = SW_KV_HEADS * HEAD_DIM
            w = jnp.concatenate([wi[:, :D_INNER], wi[:, D_INNER + 2 * kv_w:],
                                 _dup_heads(wi[:, D_INNER:D_INNER + kv_w]),
                                 _dup_heads(wi[:, D_INNER + kv_w:D_INNER + 2 * kv_w])], axis=1).astype(BF16)
            po = _proj(h, w, rope_tab, odd_groups, w.shape[1], BF16, lambda jj: jj, f"proj_odd{j}")
            oc = _wattn(po, sink_logits[j], f"win_attn{j}")
            res = _outproj([oc], w_out_odd[j].astype(BF16), xf, next_norm, last, f"outproj_odd{j}")
        if last:
            out = res[0]
        else:
            xf, h = res
    return out.reshape(BATCH, SEQ, D_MODEL)
```

```python
import functools
import math

import jax
import jax.numpy as jnp
import numpy as np
from jax import lax
from jax.experimental import pallas as pl
from jax.experimental.pallas import tpu as pltpu

D_MODEL = 1024
BATCH = 4
SEQ = 4096
DEPTH = 4
D_INNER = 2 * D_MODEL
HEAD_DIM = 64
ROPE_THETA = 500000.0
ROPE_DIM = HEAD_DIM // 4
EPS = 1e-6
DA_WIDTH = D_INNER // 2
DA_VDIM = 2 * HEAD_DIM
DA_HEADS = DA_WIDTH // DA_VDIM
HG_WIDTH = D_INNER - DA_WIDTH
HG_EXPAND = 128
HG_HEADS = HG_WIDTH // HG_EXPAND
SW_HEADS = D_INNER // HEAD_DIM
SW_KV_HEADS = SW_HEADS // 8
SW_GROUP = SW_HEADS // SW_KV_HEADS
SW_WINDOW = 128
SW_BLOCK = 128
SW_STRIP = 32

M_TOK = BATCH * SEQ
LANES = 128
F32 = jnp.float32
BF16 = jnp.bfloat16
NEG_BIG = -1e30

VMEM_LIMIT = 48 * 1024 * 1024

NORM_TM = 512
PROJ_TM = 1024
PROJ_TN = 512
OUT_TM = 512
DA_TQ = 256
DA_KC = 2048
DA_SUB = 1024
DA_STRIP = 64
HG_CHUNK = 128

Q_SCALE = HEAD_DIM ** -0.5 * math.log2(math.e)

_NT = (((1,), (1,)), ((), ()))
_TN = (((0,), (0,)), ((), ()))


def _cparams(sem):
    return pltpu.CompilerParams(dimension_semantics=sem, vmem_limit_bytes=VMEM_LIMIT)


def _rms(x, w):
    return x * lax.rsqrt(jnp.mean(x * x, axis=-1, keepdims=True) + EPS) * w


def _norm_kernel(x_ref, w_ref, h_ref):
    h_ref[...] = _rms(x_ref[...], w_ref[...]).astype(h_ref.dtype)


def _norm(x, w):
    return pl.pallas_call(
        _norm_kernel,
        grid=(M_TOK // NORM_TM,),
        in_specs=[pl.BlockSpec((NORM_TM, D_MODEL), lambda i: (i, 0)),
                  pl.BlockSpec((1, D_MODEL), lambda i: (0, 0))],
        out_specs=pl.BlockSpec((NORM_TM, D_MODEL), lambda i: (i, 0)),
        out_shape=jax.ShapeDtypeStruct((M_TOK, D_MODEL), BF16),
        compiler_params=_cparams(("parallel",)),
        name="rmsnorm0",
    )(x, w.reshape(1, D_MODEL))


def _rope_cols(acc, rope_ref, scale):
    c, s1, s2 = rope_ref[0], rope_ref[1], rope_ref[2]
    outs = []
    for t in range(acc.shape[1] // LANES):
        xs = acc[:, t * LANES:(t + 1) * LANES]
        y = xs * c + pltpu.roll(xs, LANES - ROPE_DIM // 2, 1) * s1 + pltpu.roll(xs, ROPE_DIM // 2, 1) * s2
        outs.append(y * scale if scale != 1.0 else y)
    return jnp.concatenate(outs, axis=1)


def _silu(x):
    return x * (1.0 / (1.0 + jnp.exp(-x)))


def _proj_kernel(h_ref, w_ref, rope_ref, o_ref, *, groups):
    j = pl.program_id(1)
    acc = jnp.dot(h_ref[...], w_ref[...], preferred_element_type=F32)
    for lo, hi, mode in groups:
        @pl.when((j >= lo) & (j < hi))
        def _(mode=mode):
            if mode == "rope_q":
                y = _rope_cols(acc, rope_ref, Q_SCALE)
            elif mode == "rope_k":
                y = _rope_cols(acc, rope_ref, 1.0)
            elif mode == "silu":
                y = _silu(acc)
            else:
                y = acc
            o_ref[...] = y.astype(o_ref.dtype)


def _proj(h, w, rope_tab, groups, n_out, out_dtype, col_map, name):
    n_j = n_out // PROJ_TN
    s_blocks = SEQ // PROJ_TM
    return pl.pallas_call(
        functools.partial(_proj_kernel, groups=groups),
        grid=(M_TOK // PROJ_TM, n_j),
        in_specs=[pl.BlockSpec((PROJ_TM, D_MODEL), lambda i, j: (i, 0)),
                  pl.BlockSpec((D_MODEL, PROJ_TN), lambda i, j: (0, col_map(j))),
                  pl.BlockSpec((3, PROJ_TM, LANES), lambda i, j: (0, i % s_blocks, 0))],
        out_specs=pl.BlockSpec((PROJ_TM, PROJ_TN), lambda i, j: (i, j)),
        out_shape=jax.ShapeDtypeStruct((M_TOK, n_out), out_dtype),
        compiler_params=_cparams(("parallel", "arbitrary")),
        name=name,
    )(h, w, rope_tab)


def _outproj_kernel(*refs, n_in, final):
    o_refs = refs[:n_in]
    w_ref, x_ref, nw_ref = refs[n_in:n_in + 3]
    outs = refs[n_in + 3:]
    acc = x_ref[...]
    k0 = 0
    for o_ref in o_refs:
        kw = o_ref.shape[1]
        acc = acc + jnp.dot(o_ref[...], w_ref[k0:k0 + kw, :], preferred_element_type=F32)
        k0 += kw
    y = _rms(acc, nw_ref[...])
    if final:
        outs[0][...] = y
    else:
        outs[0][...] = acc
        outs[1][...] = y.astype(BF16)


def _outproj(o_list, w, x, next_norm_w, final, name):
    n_in = len(o_list)
    row = lambda i: (i, 0)
    in_specs = [pl.BlockSpec((OUT_TM, o.shape[1]), row) for o in o_list]
    in_specs += [pl.BlockSpec((D_INNER, D_MODEL), lambda i: (0, 0)),
                 pl.BlockSpec((OUT_TM, D_MODEL), row),
                 pl.BlockSpec((1, D_MODEL), lambda i: (0, 0))]
    if final:
        out_shape = [jax.ShapeDtypeStruct((M_TOK, D_MODEL), F32)]
        out_specs = [pl.BlockSpec((OUT_TM, D_MODEL), row)]
    else:
        out_shape = [jax.ShapeDtypeStruct((M_TOK, D_MODEL), F32),
                     jax.ShapeDtypeStruct((M_TOK, D_MODEL), BF16)]
        out_specs = [pl.BlockSpec((OUT_TM, D_MODEL), row), pl.BlockSpec((OUT_TM, D_MODEL), row)]
    return pl.pallas_call(
        functools.partial(_outproj_kernel, n_in=n_in, final=final),
        grid=(M_TOK // OUT_TM,),
        in_specs=in_specs, out_specs=out_specs, out_shape=out_shape,
        compiler_params=_cparams(("parallel",)),
        name=name,
    )(*o_list, w, x, next_norm_w.reshape(1, D_MODEL))


def _dattn_step(lam_ref, sw_ref, q_ref, k_ref, v_ref, g_ref, o_ref, qm_ref, mx_ref, ls_ref, rr_ref, a_ref,
                s_in, m_in, s_out, m_out, lam_init):
    n_kc = SEQ // DA_KC
    n_sub = DA_KC // DA_SUB
    strips = [pl.ds(r * DA_STRIP, DA_STRIP) for r in range(DA_TQ // DA_STRIP)]

    def cols(sub):
        return [slice(sub * DA_SUB + t * LANES, sub * DA_SUB + (t + 1) * LANES) for t in range(DA_SUB // LANES)]

    q = q_ref[...]
    lane = lax.broadcasted_iota(jnp.int32, q.shape, 1)
    qm_ref[0] = jnp.where(lane < HEAD_DIM, q, jnp.zeros_like(q))
    qm_ref[1] = jnp.where(lane >= HEAD_DIM, q, jnp.zeros_like(q))
    mx_ref[...] = jnp.full(mx_ref.shape, NEG_BIG, F32)
    ls_ref[...] = jnp.zeros(ls_ref.shape, F32)

    def scores_and_exps(kc, _):
        k0 = pl.multiple_of(kc * DA_KC, DA_KC)
        for sub in range(n_sub):
            kb = k_ref[pl.ds(k0 + sub * DA_SUB, DA_SUB), :]
            for mp in range(2):
                s = lax.dot_general(qm_ref[mp], kb, _NT, preferred_element_type=F32)
                mx = mx_ref[mp]
                for t, tc in enumerate(cols(sub)):
                    s_in[mp, kc, :, tc] = s[:, t * LANES:(t + 1) * LANES]
                    mx = jnp.maximum(mx, s[:, t * LANES:(t + 1) * LANES])
                mx_ref[mp] = mx
            for mp in range(2):
                for rows in strips:
                    m = m_out[mp, rows, :]
                    part = ls_ref[mp, rows, :]
                    for tc in cols(sub):
                        p = jnp.exp2(s_out[mp, kc, rows, tc] - m)
                        s_out[mp, kc, rows, tc] = p
                        part = part + p
                    ls_ref[mp, rows, :] = part
        return 0

    lax.fori_loop(0, n_kc, scores_and_exps, 0)

    lp = lam_ref[...]
    lam = (jnp.exp(jnp.sum(lp[0:1] * lp[1:2], axis=1, keepdims=True))
           - jnp.exp(jnp.sum(lp[2:3] * lp[3:4], axis=1, keepdims=True)) + lam_init)
    rr_ref[0] = jnp.broadcast_to(1.0 / jnp.sum(ls_ref[0], axis=1, keepdims=True), (DA_TQ, LANES))
    rr_ref[1] = jnp.broadcast_to(lam / jnp.sum(ls_ref[1], axis=1, keepdims=True), (DA_TQ, LANES))

    acc = None
    for kc in range(n_kc):
        for sub in range(n_sub):
            for rows in strips:
                r1 = rr_ref[0, rows, :]
                r2 = rr_ref[1, rows, :]
                for tc in cols(sub):
                    a = s_out[0, kc, rows, tc] * r1 - s_out[1, kc, rows, tc] * r2
                    a_ref[kc, rows, tc] = a.astype(BF16)
        part = jnp.dot(a_ref[kc], v_ref[kc * DA_KC:(kc + 1) * DA_KC, :], preferred_element_type=F32)
        acc = part if acc is None else acc + part

    for mp in range(2):
        m_in[mp] = jnp.broadcast_to(jnp.max(mx_ref[mp], axis=1, keepdims=True), (DA_TQ, LANES))
    y = _rms(acc, sw_ref[...]) * (1.0 - lam_init)
    o_ref[...] = (y * g_ref[...].astype(F32)).astype(o_ref.dtype)


def _dattn_kernel(lam_ref, sw_ref, q_ref, k_ref, v_ref, g_ref, o_ref,
                  qm_ref, s0_ref, s1_ref, m0_ref, m1_ref, mx_ref, ls_ref, rr_ref, a_ref, *, lam_init):
    g = pl.program_id(0)
    common = (lam_ref, sw_ref, q_ref, k_ref, v_ref, g_ref, o_ref, qm_ref, mx_ref, ls_ref, rr_ref, a_ref)

    @pl.when(g == 0)
    def _():
        s1_ref[...] = jnp.zeros(s1_ref.shape, F32)
        m1_ref[...] = jnp.zeros(m1_ref.shape, F32)

    @pl.when(g % 2 == 0)
    def _():
        _dattn_step(*common, s0_ref, m0_ref, s1_ref, m1_ref, lam_init)

    @pl.when(g % 2 == 1)
    def _():
        _dattn_step(*common, s1_ref, m1_ref, s0_ref, m0_ref, lam_init)


def _dattn(pm, lam_p, subln_w, lam_init, name):
    nq = SEQ // DA_TQ
    hb = DA_WIDTH // LANES
    n_tiles = BATCH * DA_HEADS * nq

    def tile_in(g):
        return jnp.minimum(g, n_tiles - 1)

    def tile_out(g):
        return jnp.maximum(g - 1, 0)

    def bhi(t):
        return t // (DA_HEADS * nq), (t // nq) % DA_HEADS, t % nq

    def q_map(g):
        b, h, i = bhi(tile_in(g))
        return b * nq + i, h

    def k_map(g):
        b, h, _ = bhi(tile_in(g))
        return b, hb + h

    def v_map(g):
        b, h, _ = bhi(tile_out(g))
        return b, 2 * hb + h

    def g_map(g):
        b, h, i = bhi(tile_out(g))
        return b * nq + i, 3 * hb + h

    def o_map(g):
        b, h, i = bhi(tile_out(g))
        return b * nq + i, h

    n_kc = SEQ // DA_KC
    return pl.pallas_call(
        functools.partial(_dattn_kernel, lam_init=lam_init),
        grid=(n_tiles + 1,),
        in_specs=[pl.BlockSpec((4, HEAD_DIM), lambda g: (0, 0)),
                  pl.BlockSpec((1, DA_VDIM), lambda g: (0, 0)),
                  pl.BlockSpec((DA_TQ, LANES), q_map),
                  pl.BlockSpec((SEQ, LANES), k_map),
                  pl.BlockSpec((SEQ, LANES), v_map),
                  pl.BlockSpec((DA_TQ, LANES), g_map)],
        out_specs=pl.BlockSpec((DA_TQ, LANES), o_map),
        out_shape=jax.ShapeDtypeStruct((M_TOK, DA_WIDTH), BF16),
        scratch_shapes=[pltpu.VMEM((2, DA_TQ, LANES), BF16),
                        pltpu.VMEM((2, n_kc, DA_TQ, DA_KC), F32),
                        pltpu.VMEM((2, n_kc, DA_TQ, DA_KC), F32),
                        pltpu.VMEM((2, DA_TQ, LANES), F32),
                        pltpu.VMEM((2, DA_TQ, LANES), F32),
                        pltpu.VMEM((2, DA_TQ, LANES), F32),
                        pltpu.VMEM((2, DA_TQ, LANES), F32),
                        pltpu.VMEM((2, DA_TQ, LANES), F32),
                        pltpu.VMEM((n_kc, DA_TQ, DA_KC), BF16)],
        compiler_params=_cparams(("arbitrary",)),
        name=name,
    )(lam_p, subln_w.reshape(1, DA_VDIM), pm, pm, pm, pm)


def _bcast_row(p, n, off):
    c = p.shape[0]
    if n >= 16:
        pieces = [jnp.broadcast_to(p[b * n + off:b * n + off + 1, :], (n, p.shape[1])) for b in range(c // n)]
        return jnp.concatenate(pieces, axis=0) if len(pieces) > 1 else pieces[0]
    row = lax.broadcasted_iota(jnp.int32, p.shape, 0)
    d = (row % n) - off
    out = p
    for delta in range(-off, n - off):
        if delta == 0:
            continue
        out = jnp.where(d == delta, pltpu.roll(p, delta % c, 0), out)
    return out


def _hgrn_gates(z, log_lb, log1m_lb, one_m_lb):
    e = jnp.exp(-jnp.abs(z))
    d = 1.0 + e
    r = 1.0 / d
    log_sig = jnp.minimum(z, 0.0) - jnp.log(d)
    b = log1m_lb + log_sig
    log_f = jnp.maximum(log_lb, b) + jnp.log(1.0 + jnp.exp(-jnp.abs(log_lb - b)))
    key = one_m_lb * jnp.where(z >= 0.0, e * r, r)
    return log_f, key


def _hgrn_kernel(q_ref, v_ref, g_ref, zf_ref, zb_ref, lb_ref, nw_ref, o_ref,
                 acc_ref, qgb_ref, kdb_ref, decb_ref, sf_ref, sb_ref):
    C = HG_CHUNK
    n_chunks = SEQ // C
    lb = lb_ref[...]
    log_lb = jnp.log(lb)
    log1m_lb = jnp.log(1.0 - lb)
    one_m_lb = 1.0 - lb
    row = lax.broadcasted_iota(jnp.int32, (C, HG_EXPAND), 0)
    arow = lax.broadcasted_iota(jnp.int32, (C, C), 0)
    acol = lax.broadcasted_iota(jnp.int32, (C, C), 1)

    sf_ref[...] = jnp.zeros_like(sf_ref)
    sb_ref[...] = jnp.zeros_like(sb_ref)

    def fwd_chunk(n, _):
        r0 = pl.multiple_of(n * C, C)
        q = q_ref[pl.ds(r0, C), :].astype(F32)
        v = v_ref[pl.ds(r0, C), :]
        pf, kf = _hgrn_gates(zf_ref[pl.ds(r0, C), :], log_lb[0:1], log1m_lb[0:1], one_m_lb[0:1])
        pb, kb = _hgrn_gates(zb_ref[pl.ds(r0, C), :], log_lb[1:2], log1m_lb[1:2], one_m_lb[1:2])
        diag = jnp.sum(q * (kf + kb), axis=1, keepdims=True)
        a = jnp.where(arow == acol, diag, 0.0)
        zero = jnp.zeros_like(q)
        blk = 2
        while blk <= C:
            half = blk // 2
            right = (row % blk) >= half
            bf = _bcast_row(pf, blk, half - 1)
            bb = _bcast_row(pb, blk, half)
            ef = jnp.exp(jnp.where(right, pf, bf - pf))
            eb = jnp.exp(jnp.where(right, bb - pb, pb))
            qn = jnp.concatenate([q * jnp.where(right, ef, zero), q * jnp.where(right, zero, eb)], axis=1)
            kn = jnp.concatenate([kf * jnp.where(right, zero, ef), kb * jnp.where(right, eb, zero)], axis=1)
            an = lax.dot_general(qn.astype(BF16), kn.astype(BF16), _NT, preferred_element_type=F32)
            a = a + jnp.where((arow // blk) == (acol // blk), an, 0.0)
            pf = pf + jnp.where(right, bf, zero)
            pb = pb + jnp.where(right, zero, bb)
            blk *= 2
        o = jnp.dot(a.astype(BF16), v, preferred_element_type=F32)
        pf_last = pf[C - 1:C, :]
        st = sf_ref[...]
        o = o + lax.dot_general((q * jnp.exp(pf)).astype(BF16), st.astype(BF16), _NT,
                                preferred_element_type=F32)
        kdec = (kf * jnp.exp(pf_last - pf)).astype(BF16)
        sf_ref[...] = jnp.exp(pf_last) * st + lax.dot_general(v, kdec, _TN, preferred_element_type=F32)
        acc_ref[pl.ds(r0, C), :] = o
        pb_first = pb[0:1, :]
        qgb_ref[pl.ds(r0, C), :] = (q * jnp.exp(pb)).astype(BF16)
        kdb_ref[pl.ds(r0, C), :] = (kb * jnp.exp(pb_first - pb)).astype(BF16)
        decb_ref[pl.ds(n, 1), :] = jnp.exp(pb_first)
        return 0

    lax.fori_loop(0, n_chunks, fwd_chunk, 0)

    def bwd_chunk(it, _):
        n = n_chunks - 1 - it
        r0 = pl.multiple_of(n * C, C)
        v = v_ref[pl.ds(r0, C), :]
        st = sb_ref[...]
        o = acc_ref[pl.ds(r0, C), :] + lax.dot_general(qgb_ref[pl.ds(r0, C), :], st.astype(BF16), _NT,
                                                       preferred_element_type=F32)
        sb_ref[...] = decb_ref[pl.ds(n, 1), :] * st + lax.dot_general(
            v, kdb_ref[pl.ds(r0, C), :], _TN, preferred_element_type=F32)
        y = _rms(o, nw_ref[...]) * g_ref[pl.ds(r0, C), :].astype(F32)
        o_ref[pl.ds(r0, C), :] = y.astype(o_ref.dtype)
        return 0

    lax.fori_loop(0, n_chunks, bwd_chunk, 0)


def _hgrn(pm, pg, lb, norm_w, name):
    hb = HG_WIDTH // LANES
    seq_blk = lambda off: pl.BlockSpec((SEQ, LANES), lambda b, h: (b, off + h))
    return pl.pallas_call(
        _hgrn_kernel,
        grid=(BATCH, HG_HEADS),
        in_specs=[seq_blk(4 * hb), seq_blk(5 * hb), seq_blk(6 * hb), seq_blk(0), seq_blk(hb),
                  pl.BlockSpec((2, LANES), lambda b, h: (0, h)),
                  pl.BlockSpec((1, HG_EXPAND), lambda b, h: (0, 0))],
        out_specs=pl.BlockSpec((SEQ, LANES), lambda b, h: (b, h)),
        out_shape=jax.ShapeDtypeStruct((M_TOK, HG_WIDTH), BF16),
        scratch_shapes=[pltpu.VMEM((SEQ, HG_EXPAND), F32),
                        pltpu.VMEM((SEQ, HG_EXPAND), BF16),
                        pltpu.VMEM((SEQ, HG_EXPAND), BF16),
                        pltpu.VMEM((SEQ // HG_CHUNK, HG_EXPAND), F32),
                        pltpu.VMEM((HG_EXPAND, HG_EXPAND), F32),
                        pltpu.VMEM((HG_EXPAND, HG_EXPAND), F32)],
        compiler_params=_cparams(("parallel", "parallel")),
        name=name,
    )(pm, pm, pm, pg, pg, lb, norm_w.reshape(1, HG_EXPAND))


def _wattn_kernel(sink_ref, q_ref, kp_ref, kc_ref, kn_ref, vp_ref, vc_ref, vn_ref, g_ref, o_ref,
                  bias_ref, s_ref, p_ref, rden_ref):
    i = pl.program_id(1)
    band = 3 * SW_BLOCK
    pairs = SW_GROUP // 2
    qi = lax.broadcasted_iota(jnp.int32, (SW_BLOCK, band), 0)
    kj = lax.broadcasted_iota(jnp.int32, (SW_BLOCK, band), 1) - SW_BLOCK
    kpos = i * SW_BLOCK + kj
    valid = (jnp.abs(kj - qi) <= SW_WINDOW) & (kpos >= 0) & (kpos < SEQ)
    bias_ref[...] = jnp.where(valid, 0.0, -jnp.inf).astype(F32)
    low = lax.broadcasted_iota(jnp.int32, (band, LANES), 1) < HEAD_DIM
    for c in range(SW_KV_HEADS):
        slot = c % 2
        cs = slice(c * LANES, (c + 1) * LANES)
        k2 = jnp.concatenate([kp_ref[:, cs], kc_ref[:, cs], kn_ref[:, cs]], axis=0)
        v2 = jnp.concatenate([vp_ref[:, cs], vc_ref[:, cs], vn_ref[:, cs]], axis=0)
        zero = jnp.zeros_like(k2)
        kbd = jnp.concatenate([jnp.where(low, k2, zero), jnp.where(low, zero, k2)], axis=0)
        vbd = jnp.concatenate([jnp.where(low, v2, zero), jnp.where(low, zero, v2)], axis=0)
        col0 = c * SW_GROUP * HEAD_DIM
        lhs = jnp.concatenate([q_ref[:, col0 + p * LANES:col0 + (p + 1) * LANES] for p in range(pairs)], axis=0)
        s_ref[slot] = lax.dot_general(lhs, kbd, _NT, preferred_element_type=F32)
        for p in range(pairs):
            for hh in range(2):
                sink = sink_ref[c * SW_GROUP + 2 * p + hh] * math.log2(math.e)
                for st in range(SW_BLOCK // SW_STRIP):
                    rows = pl.ds(p * SW_BLOCK + st * SW_STRIP, SW_STRIP)
                    brow = pl.ds(st * SW_STRIP, SW_STRIP)
                    tiles = [s_ref[slot, rows, hh * band + t * LANES:hh * band + (t + 1) * LANES]
                             + bias_ref[brow, t * LANES:(t + 1) * LANES] for t in range(3)]
                    mx = jnp.maximum(jnp.maximum(tiles[0], tiles[1]), tiles[2])
                    m = jnp.maximum(jnp.max(mx, axis=1, keepdims=True), sink)
                    ps = [jnp.exp2(t - m) for t in tiles]
                    den = jnp.sum(ps[0] + ps[1] + ps[2], axis=1, keepdims=True) + jnp.exp2(sink - m)
                    for t in range(3):
                        p_ref[slot, rows, hh * band + t * LANES:hh * band + (t + 1) * LANES] = ps[t].astype(BF16)
                    rden_ref[slot, rows, hh * HEAD_DIM:(hh + 1) * HEAD_DIM] = jnp.broadcast_to(
                        1.0 / den, (SW_STRIP, HEAD_DIM))
        o = jnp.dot(p_ref[slot], vbd, preferred_element_type=F32) * rden_ref[slot]
        for p in range(pairs):
            col = col0 + p * LANES
            o_ref[:, col:col + LANES] = (o[p * SW_BLOCK:(p + 1) * SW_BLOCK]
                                         * g_ref[:, col:col + LANES].astype(F32)).astype(o_ref.dtype)


def _wattn(po, sink, name):
    nb = SEQ // SW_BLOCK
    kv_w = SW_KV_HEADS * LANES
    kcol = 2 * D_INNER // kv_w
    vcol = kcol + 1
    rows = (SW_GROUP // 2) * SW_BLOCK

    def kv(col, shift):
        return pl.BlockSpec((SW_BLOCK, kv_w), lambda b, i: (b * nb + jnp.clip(i + shift, 0, nb - 1), col))

    return pl.pallas_call(
        _wattn_kernel,
        grid=(BATCH, nb),
        in_specs=[pl.BlockSpec(memory_space=pltpu.SMEM),
                  pl.BlockSpec((SW_BLOCK, D_INNER), lambda b, i: (b * nb + i, 0)),
                  kv(kcol, -1), kv(kcol, 0), kv(kcol, 1),
                  kv(vcol, -1), kv(vcol, 0), kv(vcol, 1),
                  pl.BlockSpec((SW_BLOCK, D_INNER), lambda b, i: (b * nb + i, 1))],
        out_specs=pl.BlockSpec((SW_BLOCK, D_INNER), lambda b, i: (b * nb + i, 0)),
        out_shape=jax.ShapeDtypeStruct((M_TOK, D_INNER), BF16),
        scratch_shapes=[pltpu.VMEM((SW_BLOCK, 3 * SW_BLOCK), F32),
                        pltpu.VMEM((2, rows, 6 * SW_BLOCK), F32),
                        pltpu.VMEM((2, rows, 6 * SW_BLOCK), BF16),
                        pltpu.VMEM((2, rows, LANES), F32)],
        compiler_params=_cparams(("parallel", "arbitrary")),
        name=name,
    )(sink, po, po, po, po, po, po, po, po)


def _rope_tables():
    inv = ROPE_THETA ** (-jnp.arange(0, ROPE_DIM, 2, dtype=F32) / ROPE_DIM)
    ang = jnp.arange(SEQ, dtype=F32)[:, None] * inv[None, :]
    cos, sin = jnp.cos(ang), jnp.sin(ang)
    half = ROPE_DIM // 2
    m = np.arange(LANES) % HEAD_DIM
    idx = jnp.asarray(m % half)
    cos_l, sin_l = cos[:, idx], sin[:, idx]
    c = jnp.where(jnp.asarray(m < ROPE_DIM)[None, :], cos_l, 1.0)
    s1 = jnp.where(jnp.asarray(m < half)[None, :], -sin_l, 0.0)
    s2 = jnp.where(jnp.asarray((m >= half) & (m < ROPE_DIM))[None, :], sin_l, 0.0)
    return jnp.stack([c, s1, s2]).astype(F32)


def _dup_heads(w):
    d, n = w.shape[0], w.shape[1] // HEAD_DIM
    return jnp.repeat(w.reshape(d, n, 1, HEAD_DIM), 2, axis=2).reshape(d, 2 * n * HEAD_DIM)


def kernel(x, norm_even, w_in_even, lambda_qk, subln_w, hgrn_norm_w, hgrn_lb_logits, w_out_even,
           norm_odd, w_in_odd, sink_logits, w_out_odd, final_norm):
    rope_tab = _rope_tables()
    lb_cum = jnp.cumsum(jax.nn.softmax(hgrn_lb_logits.astype(F32), axis=1), axis=1)
    lb_all = lb_cum - lb_cum[:, :1]

    n_tiles = lambda width: width // PROJ_TN
    t1 = n_tiles(DA_WIDTH)
    even_groups = ((0, t1, "rope_q"), (t1, 2 * t1, "rope_k"), (2 * t1, 3 * t1, "plain"),
                   (3 * t1, 5 * t1, "silu"), (5 * t1, 6 * t1, "plain"), (6 * t1, 7 * t1, "silu"))
    n_gate_tiles = 2 * t1
    even_map = lambda j: jnp.where(j < 5 * t1, j, j + n_gate_tiles)
    gate_groups = ((0, n_gate_tiles, "plain"),)
    gate_map = lambda j: j + 5 * t1
    tq, tkv = n_tiles(D_INNER), n_tiles(2 * SW_KV_HEADS * HEAD_DIM)
    odd_groups = ((0, tq, "rope_q"), (tq, 2 * tq, "silu"), (2 * tq, 2 * tq + tkv, "rope_k"),
                  (2 * tq + tkv, 2 * tq + 2 * tkv, "plain"))

    xf = x.reshape(M_TOK, D_MODEL)
    h = _norm(xf, norm_even[0])
    out = None
    for layer in range(DEPTH):
        j = layer // 2
        last = layer == DEPTH - 1
        if last:
            next_norm = final_norm
        elif layer % 2 == 0:
            next_norm = norm_odd[j]
        else:
            next_norm = norm_even[j + 1]
        if layer % 2 == 0:
            lam_init = 0.8 - 0.6 * math.exp(-0.3 * layer)
            w = w_in_even[j].astype(BF16)
            pm = _proj(h, w, rope_tab, even_groups, 7 * DA_WIDTH, BF16, even_map, f"proj_even{j}")
            pg = _proj(h, w, rope_tab, gate_groups, 2 * HG_WIDTH, F32, gate_map, f"proj_gate{j}")
            oa = _dattn(pm, lambda_qk[j], subln_w[j], lam_init, f"diff_attn{j}")
            ob = _hgrn(pm, pg, lb_all[:, j], hgrn_norm_w[j], f"hgrn{j}")
            res = _outproj([oa, ob], w_out_even[j].astype(BF16), xf, next_norm, last, f"outproj_even{j}")
        else:
            wi = w_in_odd[j]
            kv_w = SW_KV_HEADS * HEAD_DIM
            w = jnp.concatenate([wi[:, :D_INNER], wi[:, D_INNER + 2 * kv_w:],
                                 _dup_heads(wi[:, D_INNER:D_INNER + kv_w]),
                                 _dup_heads(wi[:, D_INNER + kv_w:D_INNER + 2 * kv_w])], axis=1).astype(BF16)
            po = _proj(h, w, rope_tab, odd_groups, w.shape[1], BF16, lambda jj: jj, f"proj_odd{j}")
            oc = _wattn(po, sink_logits[j], f"win_attn{j}")
            res = _outproj([oc], w_out_odd[j].astype(BF16), xf, next_norm, last, f"outproj_odd{j}")
        if last:
            out = res[0]
        else:
            xf, h = res
    return out.reshape(BATCH, SEQ, D_MODEL)
```

```python
import functools
import math

import jax
import jax.numpy as jnp
import numpy as np
from jax import lax
from jax.experimental import pallas as pl
from jax.experimental.pallas import tpu as pltpu

D_MODEL = 1024
BATCH = 4
SEQ = 4096
DEPTH = 4
D_INNER = 2 * D_MODEL
HEAD_DIM = 64
ROPE_THETA = 500000.0
ROPE_DIM = HEAD_DIM // 4
EPS = 1e-6
DA_WIDTH = D_INNER // 2
DA_VDIM = 2 * HEAD_DIM
DA_HEADS = DA_WIDTH // DA_VDIM
HG_WIDTH = D_INNER - DA_WIDTH
HG_EXPAND = 128
HG_HEADS = HG_WIDTH // HG_EXPAND
SW_HEADS = D_INNER // HEAD_DIM
SW_KV_HEADS = SW_HEADS // 8
SW_GROUP = SW_HEADS // SW_KV_HEADS
SW_WINDOW = 128
SW_BLOCK = 128
SW_STRIP = 32

M_TOK = BATCH * SEQ
LANES = 128
F32 = jnp.float32
BF16 = jnp.bfloat16
NEG_BIG = -1e30

VMEM_LIMIT = 48 * 1024 * 1024

NORM_TM = 512
PROJ_TM = 1024
PROJ_TN = 512
PROJ_RB = 256
OUT_TM = 512
DA_TQ = 256
DA_KC = 4096
DA_SUB = 1024
DA_STRIP = 16
HG_CHUNK = 128
HG_GROUP = 8
HG_PITCH = HG_CHUNK + 8
LOG2E = math.log2(math.e)

Q_SCALE = HEAD_DIM ** -0.5 * math.log2(math.e)

_NT = (((1,), (1,)), ((), ()))
_TN = (((0,), (0,)), ((), ()))


def _cparams(sem):
    return pltpu.CompilerParams(dimension_semantics=sem, vmem_limit_bytes=VMEM_LIMIT)


def _rms(x, w):
    return x * lax.rsqrt(jnp.mean(x * x, axis=-1, keepdims=True) + EPS) * w


def _norm_kernel(x_ref, w_ref, h_ref):
    h_ref[...] = _rms(x_ref[...], w_ref[...]).astype(h_ref.dtype)


def _norm(x, w):
    return pl.pallas_call(
        _norm_kernel,
        grid=(M_TOK // NORM_TM,),
        in_specs=[pl.BlockSpec((NORM_TM, D_MODEL), lambda i: (i, 0)),
                  pl.BlockSpec((1, D_MODEL), lambda i: (0, 0))],
        out_specs=pl.BlockSpec((NORM_TM, D_MODEL), lambda i: (i, 0)),
        out_shape=jax.ShapeDtypeStruct((M_TOK, D_MODEL), BF16),
        compiler_params=_cparams(("parallel",)),
        name="rmsnorm0",
    )(x, w.reshape(1, D_MODEL))


def _rope_cols(acc, rope_ref, scale):
    c, s1, s2 = rope_ref
    outs = []
    for t in range(acc.shape[1] // LANES):
        xs = acc[:, t * LANES:(t + 1) * LANES]
        y = xs * c + pltpu.roll(xs, LANES - ROPE_DIM // 2, 1) * s1 + pltpu.roll(xs, ROPE_DIM // 2, 1) * s2
        outs.append(y * scale if scale != 1.0 else y)
    return jnp.concatenate(outs, axis=1)


def _silu(x):
    return x * (1.0 / (1.0 + jnp.exp(-x)))


def _proj_kernel(h_ref, w_ref, rope_ref, o_ref, *, groups):
    j = pl.program_id(1)
    for lo, hi, mode in groups:
        @pl.when((j >= lo) & (j < hi))
        def _(mode=mode):
            for r in range(PROJ_TM // PROJ_RB):
                rows = pl.ds(r * PROJ_RB, PROJ_RB)
                acc = jnp.dot(h_ref[rows, :], w_ref[...], preferred_element_type=F32)
                if mode in ("rope_q", "rope_k"):
                    tabs = (rope_ref[0, rows, :], rope_ref[1, rows, :], rope_ref[2, rows, :])
                    y = _rope_cols(acc, tabs, Q_SCALE if mode == "rope_q" else 1.0)
                elif mode == "silu":
                    y = _silu(acc)
                else:
                    y = acc
                o_ref[rows, :] = y.astype(o_ref.dtype)


def _proj(h, w, rope_tab, groups, n_out, out_dtype, col_map, name):
    n_j = n_out // PROJ_TN
    s_blocks = SEQ // PROJ_TM
    return pl.pallas_call(
        functools.partial(_proj_kernel, groups=groups),
        grid=(M_TOK // PROJ_TM, n_j),
        in_specs=[pl.BlockSpec((PROJ_TM, D_MODEL), lambda i, j: (i, 0)),
                  pl.BlockSpec((D_MODEL, PROJ_TN), lambda i, j: (0, col_map(j))),
                  pl.BlockSpec((3, PROJ_TM, LANES), lambda i, j: (0, i % s_blocks, 0))],
        out_specs=pl.BlockSpec((PROJ_TM, PROJ_TN), lambda i, j: (i, j)),
        out_shape=jax.ShapeDtypeStruct((M_TOK, n_out), out_dtype),
        compiler_params=_cparams(("parallel", "arbitrary")),
        name=name,
    )(h, w, rope_tab)


def _outproj_kernel(*refs, n_in, final):
    o_refs = refs[:n_in]
    w_ref, x_ref, nw_ref = refs[n_in:n_in + 3]
    outs = refs[n_in + 3:]
    acc = x_ref[...]
    k0 = 0
    for o_ref in o_refs:
        kw = o_ref.shape[1]
        acc = acc + jnp.dot(o_ref[...], w_ref[k0:k0 + kw, :], preferred_element_type=F32)
        k0 += kw
    y = _rms(acc, nw_ref[...])
    if final:
        outs[0][...] = y
    else:
        outs[0][...] = acc
        outs[1][...] = y.astype(BF16)


def _outproj(o_list, w, x, next_norm_w, final, name):
    n_in = len(o_list)
    row = lambda i: (i, 0)
    in_specs = [pl.BlockSpec((OUT_TM, o.shape[1]), row) for o in o_list]
    in_specs += [pl.BlockSpec((D_INNER, D_MODEL), lambda i: (0, 0)),
                 pl.BlockSpec((OUT_TM, D_MODEL), row),
                 pl.BlockSpec((1, D_MODEL), lambda i: (0, 0))]
    if final:
        out_shape = [jax.ShapeDtypeStruct((M_TOK, D_MODEL), F32)]
        out_specs = [pl.BlockSpec((OUT_TM, D_MODEL), row)]
    else:
        out_shape = [jax.ShapeDtypeStruct((M_TOK, D_MODEL), F32),
                     jax.ShapeDtypeStruct((M_TOK, D_MODEL), BF16)]
        out_specs = [pl.BlockSpec((OUT_TM, D_MODEL), row), pl.BlockSpec((OUT_TM, D_MODEL), row)]
    return pl.pallas_call(
        functools.partial(_outproj_kernel, n_in=n_in, final=final),
        grid=(M_TOK // OUT_TM,),
        in_specs=in_specs, out_specs=out_specs, out_shape=out_shape,
        compiler_params=_cparams(("parallel",)),
        name=name,
    )(*o_list, w, x, next_norm_w.reshape(1, D_MODEL))


def _dattn_step(lam_ref, sw_ref, q_ref, k_ref, v_ref, g_ref, o_ref, qm_ref, mx_ref, ls_ref, rr_ref, a_ref,
                s_in, m_in, s_out, m_out, lam_init):
    n_kc = SEQ // DA_KC
    n_sub = DA_KC // DA_SUB
    strips = [pl.ds(r * DA_STRIP, DA_STRIP) for r in range(DA_TQ // DA_STRIP)]

    def cols(sub):
        return [slice(sub * DA_SUB + t * LANES, sub * DA_SUB + (t + 1) * LANES) for t in range(DA_SUB // LANES)]

    q = q_ref[...]
    lane = lax.broadcasted_iota(jnp.int32, q.shape, 1)
    qm_ref[0] = jnp.where(lane < HEAD_DIM, q, jnp.zeros_like(q))
    qm_ref[1] = jnp.where(lane >= HEAD_DIM, q, jnp.zeros_like(q))
    mx_ref[...] = jnp.full(mx_ref.shape, NEG_BIG, F32)
    ls_ref[...] = jnp.zeros(ls_ref.shape, F32)

    def scores_and_exps(kc, _):
        k0 = pl.multiple_of(kc * DA_KC, DA_KC)
        for sub in range(n_sub):
            kb = k_ref[pl.ds(k0 + sub * DA_SUB, DA_SUB), :]
            for mp in range(2):
                s = lax.dot_general(qm_ref[mp], kb, _NT, preferred_element_type=F32)
                mx = mx_ref[mp]
                for t, tc in enumerate(cols(sub)):
                    s_in[mp, kc, :, tc] = s[:, t * LANES:(t + 1) * LANES]
                    mx = jnp.maximum(mx, s[:, t * LANES:(t + 1) * LANES])
                mx_ref[mp] = mx
            for mp in range(2):
                for rows in strips:
                    m = m_out[mp, rows, :]
                    part = ls_ref[mp, rows, :]
                    for tc in cols(sub):
                        p = jnp.exp2(s_out[mp, kc, rows, tc] - m)
                        s_out[mp, kc, rows, tc] = p
                        part = part + p
                    ls_ref[mp, rows, :] = part
        return 0

    lax.fori_loop(0, n_kc, scores_and_exps, 0)

    lp = lam_ref[...]
    lam = (jnp.exp(jnp.sum(lp[0:1] * lp[1:2], axis=1, keepdims=True))
           - jnp.exp(jnp.sum(lp[2:3] * lp[3:4], axis=1, keepdims=True)) + lam_init)
    l1 = jnp.sum(ls_ref[0], axis=1, keepdims=True)
    l2 = jnp.sum(ls_ref[1], axis=1, keepdims=True)
    rr_ref[...] = jnp.broadcast_to(lam * l1 / l2, (DA_TQ, LANES))

    acc = None
    for kc in range(n_kc):
        for sub in range(n_sub):
            for rows in strips:
                rho = rr_ref[rows, :]
                for tc in cols(sub):
                    a = s_out[0, kc, rows, tc] - s_out[1, kc, rows, tc] * rho
                    a_ref[kc, rows, tc] = a.astype(BF16)
        part = jnp.dot(a_ref[kc], v_ref[kc * DA_KC:(kc + 1) * DA_KC, :], preferred_element_type=F32)
        acc = part if acc is None else acc + part

    for mp in range(2):
        m_in[mp] = jnp.broadcast_to(jnp.max(mx_ref[mp], axis=1, keepdims=True), (DA_TQ, LANES))
    y = _rms(acc / l1, sw_ref[...]) * (1.0 - lam_init)
    o_ref[...] = (y * g_ref[...].astype(F32)).astype(o_ref.dtype)


def _dattn_kernel(lam_ref, sw_ref, q_ref, k_ref, v_ref, g_ref, o_ref,
                  qm_ref, s0_ref, s1_ref, m0_ref, m1_ref, mx_ref, ls_ref, rr_ref, a_ref, *, lam_init):
    g = pl.program_id(0)
    common = (lam_ref, sw_ref, q_ref, k_ref, v_ref, g_ref, o_ref, qm_ref, mx_ref, ls_ref, rr_ref, a_ref)

    @pl.when(g == 0)
    def _():
        s1_ref[...] = jnp.zeros(s1_ref.shape, F32)
        m1_ref[...] = jnp.zeros(m1_ref.shape, F32)

    @pl.when(g % 2 == 0)
    def _():
        _dattn_step(*common, s0_ref, m0_ref, s1_ref, m1_ref, lam_init)

    @pl.when(g % 2 == 1)
    def _():
        _dattn_step(*common, s1_ref, m1_ref, s0_ref, m0_ref, lam_init)


def _dattn(pm, lam_p, subln_w, lam_init, name):
    nq = SEQ // DA_TQ
    hb = DA_WIDTH // LANES
    n_tiles = BATCH * DA_HEADS * nq

    def tile_in(g):
        return jnp.minimum(g, n_tiles - 1)

    def tile_out(g):
        return jnp.maximum(g - 1, 0)

    def bhi(t):
        return t // (DA_HEADS * nq), (t // nq) % DA_HEADS, t % nq

    def q_map(g):
        b, h, i = bhi(tile_in(g))
        return b * nq + i, h

    def k_map(g):
        b, h, _ = bhi(tile_in(g))
        return b, hb + h

    def v_map(g):
        b, h, _ = bhi(tile_out(g))
        return b, 2 * hb + h

    def g_map(g):
        b, h, i = bhi(tile_out(g))
        return b * nq + i, 3 * hb + h

    def o_map(g):
        b, h, i = bhi(tile_out(g))
        return b * nq + i, h

    n_kc = SEQ // DA_KC
    return pl.pallas_call(
        functools.partial(_dattn_kernel, lam_init=lam_init),
        grid=(n_tiles + 1,),
        in_specs=[pl.BlockSpec((4, HEAD_DIM), lambda g: (0, 0)),
                  pl.BlockSpec((1, DA_VDIM), lambda g: (0, 0)),
                  pl.BlockSpec((DA_TQ, LANES), q_map),
                  pl.BlockSpec((SEQ, LANES), k_map),
                  pl.BlockSpec((SEQ, LANES), v_map),
                  pl.BlockSpec((DA_TQ, LANES), g_map)],
        out_specs=pl.BlockSpec((DA_TQ, LANES), o_map),
        out_shape=jax.ShapeDtypeStruct((M_TOK, DA_WIDTH), BF16),
        scratch_shapes=[pltpu.VMEM((2, DA_TQ, LANES), BF16),
                        pltpu.VMEM((2, n_kc, DA_TQ, DA_KC), F32),
                        pltpu.VMEM((2, n_kc, DA_TQ, DA_KC), F32),
                        pltpu.VMEM((2, DA_TQ, LANES), F32),
                        pltpu.VMEM((2, DA_TQ, LANES), F32),
                        pltpu.VMEM((2, DA_TQ, LANES), F32),
                        pltpu.VMEM((2, DA_TQ, LANES), F32),
                        pltpu.VMEM((DA_TQ, LANES), F32),
                        pltpu.VMEM((n_kc, DA_TQ, DA_KC), BF16)],
        compiler_params=_cparams(("arbitrary",)),
        name=name,
    )(lam_p, subln_w.reshape(1, DA_VDIM), pm, pm, pm, pm)


def _hgrn_gates(z, log_lb, log1m_lb, one_m_lb):
    e = jnp.exp(-jnp.abs(z))
    d = 1.0 + e
    r = 1.0 / d
    log_sig = jnp.minimum(z, 0.0) - jnp.log(d)
    b = log1m_lb + log_sig
    log_f = jnp.maximum(log_lb, b) + jnp.log(1.0 + jnp.exp(-jnp.abs(log_lb - b)))
    key = one_m_lb * jnp.where(z >= 0.0, e * r, r)
    return log_f, key


def _hgrn_kernel(q_ref, v_ref, g_ref, zf_ref, zb_ref, lb_ref, nw_ref, o_ref,
                 zfp, zbp, qp, pf, pb, kf, kb, qi, lq, lk, ks, a_ref, qgf, qgb, kdf, kdb, oi, ut, st,
                 decf, decb, mask):
    C, G, P = HG_CHUNK, HG_GROUP, HG_PITCH
    n_chunks = SEQ // C
    levels = [2 ** i for i in range(1, C.bit_length())]
    n_lv = len(levels)
    lb = lb_ref[...]
    log_lb = jnp.log(lb)
    log1m_lb = jnp.log(1.0 - lb)
    one_m_lb = 1.0 - lb

    arow = lax.broadcasted_iota(jnp.int32, (C, C), 0)
    acol = lax.broadcasted_iota(jnp.int32, (C, C), 1)
    for li, n in enumerate(levels):
        same = (arow // n) == (acol // n)
        cross = ((arow % n) >= n // 2) != ((acol % n) >= n // 2)
        mask[li] = jnp.where(same & cross, 1.0, 0.0).astype(F32)
    mask[n_lv] = jnp.where(arow == acol, 1.0, 0.0).astype(F32)

    def group_body(grp, _):
        r0 = pl.multiple_of(grp * (G * C), G * C)
        for j in range(G):
            src = pl.ds(r0 + j * C, C)
            dst = slice(j * P, j * P + C)
            zfp[dst, :] = zf_ref[src, :]
            zbp[dst, :] = zb_ref[src, :]
            qp[dst, :] = q_ref[src, :].astype(F32)
        for t in range(C):
            sl = pl.ds(t, G, stride=P)
            lf, key_f = _hgrn_gates(zfp[sl, :], log_lb[0:1], log1m_lb[0:1], one_m_lb[0:1])
            lbk, key_b = _hgrn_gates(zbp[sl, :], log_lb[1:2], log1m_lb[1:2], one_m_lb[1:2])
            pf[t] = lf * LOG2E
            pb[t] = lbk * LOG2E
            kf[t] = key_f
            kb[t] = key_b
            qi[t] = qp[sl, :]
            ks[sl, :] = key_f + key_b
        for j in range(G):
            rows = slice(j * P, j * P + C)
            a_ref[j] = mask[n_lv] * jnp.sum(qp[rows, :] * ks[rows, :], axis=1, keepdims=True)
        for li, n in enumerate(levels):
            half = n // 2
            for b0 in range(0, C, n):
                mid = b0 + half
                bf = pf[mid - 1]
                bb = pb[mid]
                for t in range(b0, mid):
                    sl = pl.ds(t, G, stride=P)
                    pbt = pb[t]
                    lk[li, sl, :] = kf[t] * jnp.exp2(bf - pf[t])
                    lq[li, sl, :] = qi[t] * jnp.exp2(pbt)
                    pb[t] = pbt + bb
                for t in range(mid, b0 + n):
                    sl = pl.ds(t, G, stride=P)
                    pft = pf[t]
                    lq[li, sl, :] = qi[t] * jnp.exp2(pft)
                    lk[li, sl, :] = kb[t] * jnp.exp2(bb - pb[t])
                    pf[t] = pft + bf
            for j in range(G):
                rows = slice(j * P, j * P + C)
                an = lax.dot_general(lq[li, rows, :].astype(BF16), lk[li, rows, :].astype(BF16), _NT,
                                     preferred_element_type=F32)
                a_ref[j] += mask[li] * an
        pf_last = pf[C - 1]
        pb_first = pb[0]
        decf[pl.ds(pl.multiple_of(grp * G, G), G), :] = jnp.exp2(pf_last)
        decb[pl.ds(pl.multiple_of(grp * G, G), G), :] = jnp.exp2(pb_first)

        for t in range(C):
            sl = pl.ds(t, G, stride=P)
            pft = pf[t]
            pbt = pb[t]
            qt = qi[t]
            qgf[grp, sl, :] = qt * jnp.exp2(pft)
            qgb[grp, sl, :] = qt * jnp.exp2(pbt)
            kdf[sl, :] = kf[t] * jnp.exp2(pf_last - pft)
            kdb[sl, :] = kb[t] * jnp.exp2(pb_first - pbt)
            if t % (C // G) == 0:
                j = t // (C // G)
                c0 = pl.multiple_of((grp * G + j) * C, C)
                oi[pl.ds(c0, C), :] = jnp.dot(a_ref[j].astype(BF16), v_ref[pl.ds(c0, C), :],
                                              preferred_element_type=F32)
        for j in range(G):
            n = grp * G + j
            rows = slice(j * P, j * P + C)
            kd = jnp.concatenate([kdf[rows, :], kdb[rows, :]], axis=1).astype(BF16)
            ut[n] = lax.dot_general(v_ref[pl.ds(pl.multiple_of(n * C, C), C), :], kd, _TN,
                                    preferred_element_type=F32)
        return 0

    lax.fori_loop(0, n_chunks // G, group_body, 0)

    def fwd_scan(n, cur):
        st[n, :, 0:HG_EXPAND] = cur.astype(BF16)
        return decf[pl.ds(n, 1), :] * cur + ut[n, :, 0:HG_EXPAND]

    lax.fori_loop(0, n_chunks, fwd_scan, jnp.zeros((HG_EXPAND, HG_EXPAND), F32))

    def bwd_scan(i, cur):
        n = n_chunks - 1 - i
        st[n, :, HG_EXPAND:2 * HG_EXPAND] = cur.astype(BF16)
        return decb[pl.ds(n, 1), :] * cur + ut[n, :, HG_EXPAND:2 * HG_EXPAND]

    lax.fori_loop(0, n_chunks, bwd_scan, jnp.zeros((HG_EXPAND, HG_EXPAND), F32))

    def out_group(grp, _):
        for j in range(G):
            n = grp * G + j
            rows = slice(j * P, j * P + C)
            c0 = pl.multiple_of(n * C, C)
            qg = jnp.concatenate([qgf[grp, rows, :], qgb[grp, rows, :]], axis=1).astype(BF16)
            o = oi[pl.ds(c0, C), :] + lax.dot_general(qg, st[n], _NT, preferred_element_type=F32)
            y = _rms(o, nw_ref[...]) * g_ref[pl.ds(c0, C), :].astype(F32)
            o_ref[pl.ds(c0, C), :] = y.astype(o_ref.dtype)
        return 0

    lax.fori_loop(0, n_chunks // G, out_group, 0)


def _hgrn(pm, pg, lb, norm_w, name):
    hb = HG_WIDTH // LANES
    dk = HG_EXPAND
    n_chunks = SEQ // HG_CHUNK
    n_groups = n_chunks // HG_GROUP
    n_lv = HG_CHUNK.bit_length() - 1
    grows = HG_GROUP * HG_PITCH
    seq_blk = lambda off: pl.BlockSpec((SEQ, LANES), lambda b, h: (b, off + h))
    return pl.pallas_call(
        _hgrn_kernel,
        grid=(BATCH, HG_HEADS),
        in_specs=[seq_blk(4 * hb), seq_blk(5 * hb), seq_blk(6 * hb), seq_blk(0), seq_blk(hb),
                  pl.BlockSpec((2, LANES), lambda b, h: (0, h)),
                  pl.BlockSpec((1, HG_EXPAND), lambda b, h: (0, 0))],
        out_specs=pl.BlockSpec((SEQ, LANES), lambda b, h: (b, h)),
        out_shape=jax.ShapeDtypeStruct((M_TOK, HG_WIDTH), BF16),
        scratch_shapes=[
            pltpu.VMEM((grows, dk), F32), pltpu.VMEM((grows, dk), F32), pltpu.VMEM((grows, dk), F32),
            pltpu.VMEM((HG_CHUNK, HG_GROUP, dk), F32), pltpu.VMEM((HG_CHUNK, HG_GROUP, dk), F32),
            pltpu.VMEM((HG_CHUNK, HG_GROUP, dk), F32), pltpu.VMEM((HG_CHUNK, HG_GROUP, dk), F32),
            pltpu.VMEM((HG_CHUNK, HG_GROUP, dk), F32),
            pltpu.VMEM((n_lv, grows, dk), F32), pltpu.VMEM((n_lv, grows, dk), F32),
            pltpu.VMEM((grows, dk), F32),
            pltpu.VMEM((HG_GROUP, HG_CHUNK, HG_CHUNK), F32),
            pltpu.VMEM((n_groups, grows, dk), F32), pltpu.VMEM((n_groups, grows, dk), F32),
            pltpu.VMEM((grows, dk), F32), pltpu.VMEM((grows, dk), F32),
            pltpu.VMEM((SEQ, dk), F32),
            pltpu.VMEM((n_chunks, dk, 2 * dk), F32), pltpu.VMEM((n_chunks, dk, 2 * dk), BF16),
            pltpu.VMEM((n_chunks, dk), F32), pltpu.VMEM((n_chunks, dk), F32),
            pltpu.VMEM((n_lv + 1, HG_CHUNK, HG_CHUNK), F32)],
        compiler_params=_cparams(("parallel", "parallel")),
        name=name,
    )(pm, pm, pm, pg, pg, lb, norm_w.reshape(1, HG_EXPAND))


def _wattn_kernel(sink_ref, q_ref, kp_ref, kc_ref, kn_ref, vp_ref, vc_ref, vn_ref, g_ref, o_ref,
                  bias_ref, s_ref, p_ref, rden_ref):
    i = pl.program_id(1)
    band = 3 * SW_BLOCK
    pairs = SW_GROUP // 2
    qi = lax.broadcasted_iota(jnp.int32, (SW_BLOCK, band), 0)
    kj = lax.broadcasted_iota(jnp.int32, (SW_BLOCK, band), 1) - SW_BLOCK
    kpos = i * SW_BLOCK + kj
    valid = (jnp.abs(kj - qi) <= SW_WINDOW) & (kpos >= 0) & (kpos < SEQ)
    bias_ref[...] = jnp.where(valid, 0.0, -jnp.inf).astype(F32)
    low = lax.broadcasted_iota(jnp.int32, (band, LANES), 1) < HEAD_DIM
    for c in range(SW_KV_HEADS):
        slot = c % 2
        cs = slice(c * LANES, (c + 1) * LANES)
        k2 = jnp.concatenate([kp_ref[:, cs], kc_ref[:, cs], kn_ref[:, cs]], axis=0)
        v2 = jnp.concatenate([vp_ref[:, cs], vc_ref[:, cs], vn_ref[:, cs]], axis=0)
        zero = jnp.zeros_like(k2)
        kbd = jnp.concatenate([jnp.where(low, k2, zero), jnp.where(low, zero, k2)], axis=0)
        vbd = jnp.concatenate([jnp.where(low, v2, zero), jnp.where(low, zero, v2)], axis=0)
        col0 = c * SW_GROUP * HEAD_DIM
        lhs = jnp.concatenate([q_ref[:, col0 + p * LANES:col0 + (p + 1) * LANES] for p in range(pairs)], axis=0)
        s_ref[slot] = lax.dot_general(lhs, kbd, _NT, preferred_element_type=F32)
        for p in range(pairs):
            for hh in range(2):
                sink = sink_ref[c * SW_GROUP + 2 * p + hh] * math.log2(math.e)
                for st in range(SW_BLOCK // SW_STRIP):
                    rows = pl.ds(p * SW_BLOCK + st * SW_STRIP, SW_STRIP)
                    brow = pl.ds(st * SW_STRIP, SW_STRIP)
                    tiles = [s_ref[slot, rows, hh * band + t * LANES:hh * band + (t + 1) * LANES]
                             + bias_ref[brow, t * LANES:(t + 1) * LANES] for t in range(3)]
                    mx = jnp.maximum(jnp.maximum(tiles[0], tiles[1]), tiles[2])
                    m = jnp.maximum(jnp.max(mx, axis=1, keepdims=True), sink)
                    ps = [jnp.exp2(t - m) for t in tiles]
                    den = jnp.sum(ps[0] + ps[1] + ps[2], axis=1, keepdims=True) + jnp.exp2(sink - m)
                    for t in range(3):
                        p_ref[slot, rows, hh * band + t * LANES:hh * band + (t + 1) * LANES] = ps[t].astype(BF16)
                    rden_ref[slot, rows, hh * HEAD_DIM:(hh + 1) * HEAD_DIM] = jnp.broadcast_to(
                        1.0 / den, (SW_STRIP, HEAD_DIM))
        o = jnp.dot(p_ref[slot], vbd, preferred_element_type=F32) * rden_ref[slot]
        for p in range(pairs):
            col = col0 + p * LANES
            o_ref[:, col:col + LANES] = (o[p * SW_BLOCK:(p + 1) * SW_BLOCK]
                                         * g_ref[:, col:col + LANES].astype(F32)).astype(o_ref.dtype)


def _wattn(po, sink, name):
    nb = SEQ // SW_BLOCK
    kv_w = SW_KV_HEADS * LANES
    kcol = 2 * D_INNER // kv_w
    vcol = kcol + 1
    rows = (SW_GROUP // 2) * SW_BLOCK

    def kv(col, shift):
        return pl.BlockSpec((SW_BLOCK, kv_w), lambda b, i: (b * nb + jnp.clip(i + shift, 0, nb - 1), col))

    return pl.pallas_call(
        _wattn_kernel,
        grid=(BATCH, nb),
        in_specs=[pl.BlockSpec(memory_space=pltpu.SMEM),
                  pl.BlockSpec((SW_BLOCK, D_INNER), lambda b, i: (b * nb + i, 0)),
                  kv(kcol, -1), kv(kcol, 0), kv(kcol, 1),
                  kv(vcol, -1), kv(vcol, 0), kv(vcol, 1),
                  pl.BlockSpec((SW_BLOCK, D_INNER), lambda b, i: (b * nb + i, 1))],
        out_specs=pl.BlockSpec((SW_BLOCK, D_INNER), lambda b, i: (b * nb + i, 0)),
        out_shape=jax.ShapeDtypeStruct((M_TOK, D_INNER), BF16),
        scratch_shapes=[pltpu.VMEM((SW_BLOCK, 3 * SW_BLOCK), F32),
                        pltpu.VMEM((2, rows, 6 * SW_BLOCK), F32),
                        pltpu.VMEM((2, rows, 6 * SW_BLOCK), BF16),
                        pltpu.VMEM((2, rows, LANES), F32)],
        compiler_params=_cparams(("parallel", "arbitrary")),
        name=name,
    )(sink, po, po, po, po, po, po, po, po)


def _rope_tables():
    inv = ROPE_THETA ** (-jnp.arange(0, ROPE_DIM, 2, dtype=F32) / ROPE_DIM)
    ang = jnp.arange(SEQ, dtype=F32)[:, None] * inv[None, :]
    cos, sin = jnp.cos(ang), jnp.sin(ang)
    half = ROPE_DIM // 2
    m = np.arange(LANES) % HEAD_DIM
    idx = jnp.asarray(m % half)
    cos_l, sin_l = cos[:, idx], sin[:, idx]
    c = jnp.where(jnp.asarray(m < ROPE_DIM)[None, :], cos_l, 1.0)
    s1 = jnp.where(jnp.asarray(m < half)[None, :], -sin_l, 0.0)
    s2 = jnp.where(jnp.asarray((m >= half) & (m < ROPE_DIM))[None, :], sin_l, 0.0)
    return jnp.stack([c, s1, s2]).astype(F32)


def _dup_heads(w):
    d, n = w.shape[0], w.shape[1] // HEAD_DIM
    return jnp.repeat(w.reshape(d, n, 1, HEAD_DIM), 2, axis=2).reshape(d, 2 * n * HEAD_DIM)


def kernel(x, norm_even, w_in_even, lambda_qk, subln_w, hgrn_norm_w, hgrn_lb_logits, w_out_even,
           norm_odd, w_in_odd, sink_logits, w_out_odd, final_norm):
    rope_tab = _rope_tables()
    lb_cum = jnp.cumsum(jax.nn.softmax(hgrn_lb_logits.astype(F32), axis=1), axis=1)
    lb_all = lb_cum - lb_cum[:, :1]

    n_tiles = lambda width: width // PROJ_TN
    t1 = n_tiles(DA_WIDTH)
    even_groups = ((0, t1, "rope_q"), (t1, 2 * t1, "rope_k"), (2 * t1, 3 * t1, "plain"),
                   (3 * t1, 5 * t1, "silu"), (5 * t1, 6 * t1, "plain"), (6 * t1, 7 * t1, "silu"))
    n_gate_tiles = 2 * t1
    even_map = lambda j: jnp.where(j < 5 * t1, j, j + n_gate_tiles)
    gate_groups = ((0, n_gate_tiles, "plain"),)
    gate_map = lambda j: j + 5 * t1
    tq, tkv = n_tiles(D_INNER), n_tiles(2 * SW_KV_HEADS * HEAD_DIM)
    odd_groups = ((0, tq, "rope_q"), (tq, 2 * tq, "silu"), (2 * tq, 2 * tq + tkv, "rope_k"),
                  (2 * tq + tkv, 2 * tq + 2 * tkv, "plain"))

    xf = x.reshape(M_TOK, D_MODEL)
    h = _norm(xf, norm_even[0])
    out = None
    for layer in range(DEPTH):
        j = layer // 2
        last = layer == DEPTH - 1
        if last:
            next_norm = final_norm
        elif layer % 2 == 0:
            next_norm = norm_odd[j]
        else:
            next_norm = norm_even[j + 1]
        if layer % 2 == 0:
            lam_init = 0.8 - 0.6 * math.exp(-0.3 * layer)
            w = w_in_even[j].astype(BF16)
            pm = _proj(h, w, rope_tab, even_groups, 7 * DA_WIDTH, BF16, even_map, f"proj_even{j}")
            pg = _proj(h, w, rope_tab, gate_groups, 2 * HG_WIDTH, F32, gate_map, f"proj_gate{j}")
            oa = _dattn(pm, lambda_qk[j], subln_w[j], lam_init, f"diff_attn{j}")
            ob = _hgrn(pm, pg, lb_all[:, j], hgrn_norm_w[j], f"hgrn{j}")
            res = _outproj([oa, ob], w_out_even[j].astype(BF16), xf, next_norm, last, f"outproj_even{j}")
        else:
            wi = w_in_odd[j]
            kv_w = SW_KV_HEADS * HEAD_DIM
            w = jnp.concatenate([wi[:, :D_INNER], wi[:, D_INNER + 2 * kv_w:],
                                 _dup_heads(wi[:, D_INNER:D_INNER + kv_w]),
                                 _dup_heads(wi[:, D_INNER + kv_w:D_INNER + 2 * kv_w])], axis=1).astype(BF16)
            po = _proj(h, w, rope_tab, odd_groups, w.shape[1], BF16, lambda jj: jj, f"proj_odd{j}")
            oc = _wattn(po, sink_logits[j], f"win_attn{j}")
            res = _outproj([oc], w_out_odd[j].astype(BF16), xf, next_norm, last, f"outproj_odd{j}")
        if last:
            out = res[0]
        else:
            xf, h = res
    return out.reshape(BATCH, SEQ, D_MODEL)
```

```python
import functools
import math

import jax
import jax.numpy as jnp
import numpy as np
from jax import lax
from jax.experimental import pallas as pl
from jax.experimental.pallas import tpu as pltpu

D_MODEL = 1024
BATCH = 4
SEQ = 4096
DEPTH = 4
D_INNER = 2 * D_MODEL
HEAD_DIM = 64
ROPE_THETA = 500000.0
ROPE_DIM = HEAD_DIM // 4
EPS = 1e-6
DA_WIDTH = D_INNER // 2
DA_VDIM = 2 * HEAD_DIM
DA_HEADS = DA_WIDTH // DA_VDIM
HG_WIDTH = D_INNER - DA_WIDTH
HG_EXPAND = 128
HG_HEADS = HG_WIDTH // HG_EXPAND
SW_HEADS = D_INNER // HEAD_DIM
SW_KV_HEADS = SW_HEADS // 8
SW_GROUP = SW_HEADS // SW_KV_HEADS
SW_WINDOW = 128
SW_BLOCK = 128
SW_STRIP = 32

M_TOK = BATCH * SEQ
LANES = 128
F32 = jnp.float32
BF16 = jnp.bfloat16
NEG_BIG = -1e30

VMEM_LIMIT = 48 * 1024 * 1024

NORM_TM = 512
PROJ_TM = 1024
PROJ_TN = 512
PROJ_RB = 256
OUT_TM = 512
DA_TQ = 256
DA_SUBS = (1024, 1024, 1024, 1024)
DA_RC = 128
HG_CHUNK = 128
HG_GROUP = 8
HG_PITCH = HG_CHUNK + 8
LOG2E = math.log2(math.e)

Q_SCALE = HEAD_DIM ** -0.5 * math.log2(math.e)

_NT = (((1,), (1,)), ((), ()))
_TN = (((0,), (0,)), ((), ()))


def _cparams(sem):
    return pltpu.CompilerParams(dimension_semantics=sem, vmem_limit_bytes=VMEM_LIMIT)


def _rms(x, w):
    return x * lax.rsqrt(jnp.mean(x * x, axis=-1, keepdims=True) + EPS) * w


def _norm_kernel(x_ref, w_ref, h_ref):
    h_ref[...] = _rms(x_ref[...], w_ref[...]).astype(h_ref.dtype)


def _norm(x, w):
    return pl.pallas_call(
        _norm_kernel,
        grid=(M_TOK // NORM_TM,),
        in_specs=[pl.BlockSpec((NORM_TM, D_MODEL), lambda i: (i, 0)),
                  pl.BlockSpec((1, D_MODEL), lambda i: (0, 0))],
        out_specs=pl.BlockSpec((NORM_TM, D_MODEL), lambda i: (i, 0)),
        out_shape=jax.ShapeDtypeStruct((M_TOK, D_MODEL), BF16),
        compiler_params=_cparams(("parallel",)),
        name="rmsnorm0",
    )(x, w.reshape(1, D_MODEL))


def _rope_cols(acc, rope_ref, scale):
    c, s1, s2 = rope_ref
    outs = []
    for t in range(acc.shape[1] // LANES):
        xs = acc[:, t * LANES:(t + 1) * LANES]
        y = xs * c + pltpu.roll(xs, LANES - ROPE_DIM // 2, 1) * s1 + pltpu.roll(xs, ROPE_DIM // 2, 1) * s2
        outs.append(y * scale if scale != 1.0 else y)
    return jnp.concatenate(outs, axis=1)


def _silu(x):
    return x * (1.0 / (1.0 + jnp.exp(-x)))


def _proj_kernel(h_ref, w_ref, rope_ref, o_ref, *, groups):
    j = pl.program_id(1)
    for lo, hi, mode in groups:
        @pl.when((j >= lo) & (j < hi))
        def _(mode=mode):
            for r in range(PROJ_TM // PROJ_RB):
                rows = pl.ds(r * PROJ_RB, PROJ_RB)
                acc = jnp.dot(h_ref[rows, :], w_ref[...], preferred_element_type=F32)
                if mode in ("rope_q", "rope_k"):
                    tabs = (rope_ref[0, rows, :], rope_ref[1, rows, :], rope_ref[2, rows, :])
                    y = _rope_cols(acc, tabs, Q_SCALE if mode == "rope_q" else 1.0)
                elif mode == "silu":
                    y = _silu(acc)
                else:
                    y = acc
                o_ref[rows, :] = y.astype(o_ref.dtype)


def _proj(h, w, rope_tab, groups, n_out, out_dtype, col_map, name):
    n_j = n_out // PROJ_TN
    s_blocks = SEQ // PROJ_TM
    return pl.pallas_call(
        functools.partial(_proj_kernel, groups=groups),
        grid=(M_TOK // PROJ_TM, n_j),
        in_specs=[pl.BlockSpec((PROJ_TM, D_MODEL), lambda i, j: (i, 0)),
                  pl.BlockSpec((D_MODEL, PROJ_TN), lambda i, j: (0, col_map(j))),
                  pl.BlockSpec((3, PROJ_TM, LANES), lambda i, j: (0, i % s_blocks, 0))],
        out_specs=pl.BlockSpec((PROJ_TM, PROJ_TN), lambda i, j: (i, j)),
        out_shape=jax.ShapeDtypeStruct((M_TOK, n_out), out_dtype),
        compiler_params=_cparams(("parallel", "arbitrary")),
        name=name,
    )(h, w, rope_tab)


def _outproj_kernel(*refs, n_in, final):
    o_refs = refs[:n_in]
    w_ref, x_ref, nw_ref = refs[n_in:n_in + 3]
    outs = refs[n_in + 3:]
    acc = x_ref[...]
    k0 = 0
    for o_ref in o_refs:
        kw = o_ref.shape[1]
        acc = acc + jnp.dot(o_ref[...], w_ref[k0:k0 + kw, :], preferred_element_type=F32)
        k0 += kw
    y = _rms(acc, nw_ref[...])
    if final:
        outs[0][...] = y
    else:
        outs[0][...] = acc
        outs[1][...] = y.astype(BF16)


def _outproj(o_list, w, x, next_norm_w, final, name):
    n_in = len(o_list)
    row = lambda i: (i, 0)
    in_specs = [pl.BlockSpec((OUT_TM, o.shape[1]), row) for o in o_list]
    in_specs += [pl.BlockSpec((D_INNER, D_MODEL), lambda i: (0, 0)),
                 pl.BlockSpec((OUT_TM, D_MODEL), row),
                 pl.BlockSpec((1, D_MODEL), lambda i: (0, 0))]
    if final:
        out_shape = [jax.ShapeDtypeStruct((M_TOK, D_MODEL), F32)]
        out_specs = [pl.BlockSpec((OUT_TM, D_MODEL), row)]
    else:
        out_shape = [jax.ShapeDtypeStruct((M_TOK, D_MODEL), F32),
                     jax.ShapeDtypeStruct((M_TOK, D_MODEL), BF16)]
        out_specs = [pl.BlockSpec((OUT_TM, D_MODEL), row), pl.BlockSpec((OUT_TM, D_MODEL), row)]
    return pl.pallas_call(
        functools.partial(_outproj_kernel, n_in=n_in, final=final),
        grid=(M_TOK // OUT_TM,),
        in_specs=in_specs, out_specs=out_specs, out_shape=out_shape,
        compiler_params=_cparams(("parallel",)),
        name=name,
    )(*o_list, w, x, next_norm_w.reshape(1, D_MODEL))


def _dattn_step(lam_ref, sw_ref, q_ref, k_ref, v_ref, g_ref, o_ref, qm_ref, mx_ref, ls_ref, a_ref, ot_ref, l1_ref,
                s_in, m_in, s_out, m_out, lam_init):
    bounds = np.cumsum((0,) + DA_SUBS)
    assert bounds[-1] == SEQ
    sub8 = (DA_RC // 8, 8, DA_TQ)

    ot = ot_ref[...] / l1_ref[0:1, :]
    yt = ot * lax.rsqrt(jnp.mean(ot * ot, axis=0, keepdims=True) + EPS) * sw_ref[...] * (1.0 - lam_init)
    o_ref[...] = (yt.T * g_ref[...].astype(F32)).astype(o_ref.dtype)

    q = q_ref[...]
    lane = lax.broadcasted_iota(jnp.int32, q.shape, 1)
    qm_ref[0] = jnp.where(lane < HEAD_DIM, q, jnp.zeros_like(q))
    qm_ref[1] = jnp.where(lane >= HEAD_DIM, q, jnp.zeros_like(q))
    mx_ref[...] = jnp.full(mx_ref.shape, NEG_BIG, F32)
    ls_ref[...] = jnp.zeros(ls_ref.shape, F32)

    for lo, hi in zip(bounds[:-1], bounds[1:]):
        lo, hi = int(lo), int(hi)
        kb = k_ref[lo:hi, :]
        for mp in range(2):
            st = lax.dot_general(kb, qm_ref[mp], _NT, preferred_element_type=F32)
            s_in[mp, lo:hi, :] = st
            mx_ref[mp] = jnp.maximum(mx_ref[mp], jnp.max(st.reshape((hi - lo) // 8, 8, DA_TQ), axis=0))
        for mp in range(2):
            m = m_out[mp]
            part = ls_ref[mp]
            for r0 in range(lo, hi, DA_RC):
                rows = slice(r0, r0 + DA_RC)
                p = jnp.exp2(s_out[mp, rows, :].reshape(sub8) - m[None])
                s_out[mp, rows, :] = p.reshape(DA_RC, DA_TQ)
                part = part + jnp.sum(p, axis=0)
            ls_ref[mp] = part

    lp = lam_ref[...]
    lam = (jnp.exp(jnp.sum(lp[0:1] * lp[1:2], axis=1, keepdims=True))
           - jnp.exp(jnp.sum(lp[2:3] * lp[3:4], axis=1, keepdims=True)) + lam_init)
    l1 = jnp.sum(ls_ref[0], axis=0, keepdims=True)
    l2 = jnp.sum(ls_ref[1], axis=0, keepdims=True)
    rho = jnp.broadcast_to(lam * l1 / l2, (8, DA_TQ))
    for r0 in range(0, SEQ, DA_RC):
        rows = slice(r0, r0 + DA_RC)
        a = s_out[0, rows, :].reshape(sub8) - s_out[1, rows, :].reshape(sub8) * rho[None]
        a_ref[rows, :] = a.reshape(DA_RC, DA_TQ).astype(BF16)
    ot_ref[...] = lax.dot_general(v_ref[...], a_ref[...], _TN, preferred_element_type=F32)
    l1_ref[...] = jnp.broadcast_to(l1, (8, DA_TQ))
    for mp in range(2):
        m_in[mp] = jnp.broadcast_to(jnp.max(mx_ref[mp], axis=0, keepdims=True), (8, DA_TQ))


def _dattn_kernel(lam_ref, sw_ref, q_ref, k_ref, v_ref, g_ref, o_ref,
                  qm_ref, s0_ref, s1_ref, m0_ref, m1_ref, mx_ref, ls_ref, a_ref, ot_ref, l1_ref, *, lam_init):
    g = pl.program_id(0)
    common = (lam_ref, sw_ref, q_ref, k_ref, v_ref, g_ref, o_ref, qm_ref, mx_ref, ls_ref, a_ref, ot_ref, l1_ref)

    @pl.when(g == 0)
    def _():
        s1_ref[...] = jnp.zeros(s1_ref.shape, F32)
        m1_ref[...] = jnp.zeros(m1_ref.shape, F32)
        ot_ref[...] = jnp.zeros(ot_ref.shape, F32)
        l1_ref[...] = jnp.ones(l1_ref.shape, F32)

    @pl.when(g % 2 == 0)
    def _():
        _dattn_step(*common, s0_ref, m0_ref, s1_ref, m1_ref, lam_init)

    @pl.when(g % 2 == 1)
    def _():
        _dattn_step(*common, s1_ref, m1_ref, s0_ref, m0_ref, lam_init)


def _dattn(pm, lam_p, subln_w, lam_init, name):
    nq = SEQ // DA_TQ
    hb = DA_WIDTH // LANES
    n_tiles = BATCH * DA_HEADS * nq

    def tile_in(g):
        return jnp.minimum(g, n_tiles - 1)

    def tile_out(g):
        return jnp.clip(g - 1, 0, n_tiles - 1)

    def tile_fin(g):
        return jnp.clip(g - 2, 0, n_tiles - 1)

    def bhi(t):
        return t // (DA_HEADS * nq), (t // nq) % DA_HEADS, t % nq

    def q_map(g):
        b, h, i = bhi(tile_in(g))
        return b * nq + i, h

    def k_map(g):
        b, h, _ = bhi(tile_in(g))
        return b, hb + h

    def v_map(g):
        b, h, _ = bhi(tile_out(g))
        return b, 2 * hb + h

    def g_map(g):
        b, h, i = bhi(tile_fin(g))
        return b * nq + i, 3 * hb + h

    def o_map(g):
        b, h, i = bhi(tile_fin(g))
        return b * nq + i, h

    return pl.pallas_call(
        functools.partial(_dattn_kernel, lam_init=lam_init),
        grid=(n_tiles + 2,),
        in_specs=[pl.BlockSpec((4, HEAD_DIM), lambda g: (0, 0)),
                  pl.BlockSpec((DA_VDIM, 1), lambda g: (0, 0)),
                  pl.BlockSpec((DA_TQ, LANES), q_map),
                  pl.BlockSpec((SEQ, LANES), k_map),
                  pl.BlockSpec((SEQ, LANES), v_map),
                  pl.BlockSpec((DA_TQ, LANES), g_map)],
        out_specs=pl.BlockSpec((DA_TQ, LANES), o_map),
        out_shape=jax.ShapeDtypeStruct((M_TOK, DA_WIDTH), BF16),
        scratch_shapes=[pltpu.VMEM((2, DA_TQ, LANES), BF16),
                        pltpu.VMEM((2, SEQ, DA_TQ), F32),
                        pltpu.VMEM((2, SEQ, DA_TQ), F32),
                        pltpu.VMEM((2, 8, DA_TQ), F32),
                        pltpu.VMEM((2, 8, DA_TQ), F32),
                        pltpu.VMEM((2, 8, DA_TQ), F32),
                        pltpu.VMEM((2, 8, DA_TQ), F32),
                        pltpu.VMEM((SEQ, DA_TQ), BF16),
                        pltpu.VMEM((DA_VDIM, DA_TQ), F32),
                        pltpu.VMEM((8, DA_TQ), F32)],
        compiler_params=_cparams(("arbitrary",)),
        name=name,
    )(lam_p, subln_w.reshape(DA_VDIM, 1), pm, pm, pm, pm)


def _hgrn_gates(z, log_lb, log1m_lb, one_m_lb):
    e = jnp.exp(-jnp.abs(z))
    d = 1.0 + e
    r = 1.0 / d
    log_sig = jnp.minimum(z, 0.0) - jnp.log(d)
    b = log1m_lb + log_sig
    log_f = jnp.maximum(log_lb, b) + jnp.log(1.0 + jnp.exp(-jnp.abs(log_lb - b)))
    key = one_m_lb * jnp.where(z >= 0.0, e * r, r)
    return log_f, key


def _hgrn_kernel(q_ref, v_ref, g_ref, zf_ref, zb_ref, lb_ref, nw_ref, o_ref,
                 zfp, zbp, qp, pf, pb, kf, kb, qi, lq, lk, ks, a_ref, qgf, qgb, kdf, kdb, oi, ut, st,
                 decf, decb, mask):
    C, G, P = HG_CHUNK, HG_GROUP, HG_PITCH
    n_chunks = SEQ // C
    levels = [2 ** i for i in range(1, C.bit_length())]
    n_lv = len(levels)
    lb = lb_ref[...]
    log_lb = jnp.log(lb)
    log1m_lb = jnp.log(1.0 - lb)
    one_m_lb = 1.0 - lb

    arow = lax.broadcasted_iota(jnp.int32, (C, C), 0)
    acol = lax.broadcasted_iota(jnp.int32, (C, C), 1)
    for li, n in enumerate(levels):
        same = (arow // n) == (acol // n)
        cross = ((arow % n) >= n // 2) != ((acol % n) >= n // 2)
        mask[li] = jnp.where(same & cross, 1.0, 0.0).astype(F32)
    mask[n_lv] = jnp.where(arow == acol, 1.0, 0.0).astype(F32)

    def group_body(grp, _):
        r0 = pl.multiple_of(grp * (G * C), G * C)
        for j in range(G):
            src = pl.ds(r0 + j * C, C)
            dst = slice(j * P, j * P + C)
            zfp[dst, :] = zf_ref[src, :]
            zbp[dst, :] = zb_ref[src, :]
            qp[dst, :] = q_ref[src, :].astype(F32)
        for t in range(C):
            sl = pl.ds(t, G, stride=P)
            lf, key_f = _hgrn_gates(zfp[sl, :], log_lb[0:1], log1m_lb[0:1], one_m_lb[0:1])
            lbk, key_b = _hgrn_gates(zbp[sl, :], log_lb[1:2], log1m_lb[1:2], one_m_lb[1:2])
            pf[t] = lf * LOG2E
            pb[t] = lbk * LOG2E
            kf[t] = key_f
            kb[t] = key_b
            qi[t] = qp[sl, :]
            ks[sl, :] = key_f + key_b
        for j in range(G):
            rows = slice(j * P, j * P + C)
            a_ref[j] = mask[n_lv] * jnp.sum(qp[rows, :] * ks[rows, :], axis=1, keepdims=True)
        for li, n in enumerate(levels):
            half = n // 2
            for b0 in range(0, C, n):
                mid = b0 + half
                bf = pf[mid - 1]
                bb = pb[mid]
                for t in range(b0, mid):
                    sl = pl.ds(t, G, stride=P)
                    pbt = pb[t]
                    lk[li, sl, :] = kf[t] * jnp.exp2(bf - pf[t])
                    lq[li, sl, :] = qi[t] * jnp.exp2(pbt)
                    pb[t] = pbt + bb
                for t in range(mid, b0 + n):
                    sl = pl.ds(t, G, stride=P)
                    pft = pf[t]
                    lq[li, sl, :] = qi[t] * jnp.exp2(pft)
                    lk[li, sl, :] = kb[t] * jnp.exp2(bb - pb[t])
                    pf[t] = pft + bf
            for j in range(G):
                rows = slice(j * P, j * P + C)
                an = lax.dot_general(lq[li, rows, :].astype(BF16), lk[li, rows, :].astype(BF16), _NT,
                                     preferred_element_type=F32)
                a_ref[j] += mask[li] * an
        pf_last = pf[C - 1]
        pb_first = pb[0]
        decf[pl.ds(pl.multiple_of(grp * G, G), G), :] = jnp.exp2(pf_last)
        decb[pl.ds(pl.multiple_of(grp * G, G), G), :] = jnp.exp2(pb_first)

        for t in range(C):
            sl = pl.ds(t, G, stride=P)
            pft = pf[t]
            pbt = pb[t]
            qt = qi[t]
            qgf[grp, sl, :] = qt * jnp.exp2(pft)
            qgb[grp, sl, :] = qt * jnp.exp2(pbt)
            kdf[sl, :] = kf[t] * jnp.exp2(pf_last - pft)
            kdb[sl, :] = kb[t] * jnp.exp2(pb_first - pbt)
            if t % (C // G) == 0:
                j = t // (C // G)
                c0 = pl.multiple_of((grp * G + j) * C, C)
                oi[pl.ds(c0, C), :] = jnp.dot(a_ref[j].astype(BF16), v_ref[pl.ds(c0, C), :],
                                              preferred_element_type=F32)
        for j in range(G):
            n = grp * G + j
            rows = slice(j * P, j * P + C)
            kd = jnp.concatenate([kdf[rows, :], kdb[rows, :]], axis=1).astype(BF16)
            ut[n] = lax.dot_general(v_ref[pl.ds(pl.multiple_of(n * C, C), C), :], kd, _TN,
                                    preferred_element_type=F32)
        return 0

    lax.fori_loop(0, n_chunks // G, group_body, 0)

    def fwd_scan(n, cur):
        st[n, :, 0:HG_EXPAND] = cur.astype(BF16)
        return decf[pl.ds(n, 1), :] * cur + ut[n, :, 0:HG_EXPAND]

    lax.fori_loop(0, n_chunks, fwd_scan, jnp.zeros((HG_EXPAND, HG_EXPAND), F32))

    def bwd_scan(i, cur):
        n = n_chunks - 1 - i
        st[n, :, HG_EXPAND:2 * HG_EXPAND] = cur.astype(BF16)
        return decb[pl.ds(n, 1), :] * cur + ut[n, :, HG_EXPAND:2 * HG_EXPAND]

    lax.fori_loop(0, n_chunks, bwd_scan, jnp.zeros((HG_EXPAND, HG_EXPAND), F32))

    def out_group(grp, _):
        for j in range(G):
            n = grp * G + j
            rows = slice(j * P, j * P + C)
            c0 = pl.multiple_of(n * C, C)
            qg = jnp.concatenate([qgf[grp, rows, :], qgb[grp, rows, :]], axis=1).astype(BF16)
            o = oi[pl.ds(c0, C), :] + lax.dot_general(qg, st[n], _NT, preferred_element_type=F32)
            y = _rms(o, nw_ref[...]) * g_ref[pl.ds(c0, C), :].astype(F32)
            o_ref[pl.ds(c0, C), :] = y.astype(o_ref.dtype)
        return 0

    lax.fori_loop(0, n_chunks // G, out_group, 0)


def _hgrn(pm, pg, lb, norm_w, name):
    hb = HG_WIDTH // LANES
    dk = HG_EXPAND
    n_chunks = SEQ // HG_CHUNK
    n_groups = n_chunks // HG_GROUP
    n_lv = HG_CHUNK.bit_length() - 1
    grows = HG_GROUP * HG_PITCH
    seq_blk = lambda off: pl.BlockSpec((SEQ, LANES), lambda b, h: (b, off + h))
    return pl.pallas_call(
        _hgrn_kernel,
        grid=(BATCH, HG_HEADS),
        in_specs=[seq_blk(4 * hb), seq_blk(5 * hb), seq_blk(6 * hb), seq_blk(0), seq_blk(hb),
                  pl.BlockSpec((2, LANES), lambda b, h: (0, h)),
                  pl.BlockSpec((1, HG_EXPAND), lambda b, h: (0, 0))],
        out_specs=pl.BlockSpec((SEQ, LANES), lambda b, h: (b, h)),
        out_shape=jax.ShapeDtypeStruct((M_TOK, HG_WIDTH), BF16),
        scratch_shapes=[
            pltpu.VMEM((grows, dk), F32), pltpu.VMEM((grows, dk), F32), pltpu.VMEM((grows, dk), F32),
            pltpu.VMEM((HG_CHUNK, HG_GROUP, dk), F32), pltpu.VMEM((HG_CHUNK, HG_GROUP, dk), F32),
            pltpu.VMEM((HG_CHUNK, HG_GROUP, dk), F32), pltpu.VMEM((HG_CHUNK, HG_GROUP, dk), F32),
            pltpu.VMEM((HG_CHUNK, HG_GROUP, dk), F32),
            pltpu.VMEM((n_lv, grows, dk), F32), pltpu.VMEM((n_lv, grows, dk), F32),
            pltpu.VMEM((grows, dk), F32),
            pltpu.VMEM((HG_GROUP, HG_CHUNK, HG_CHUNK), F32),
            pltpu.VMEM((n_groups, grows, dk), F32), pltpu.VMEM((n_groups, grows, dk), F32),
            pltpu.VMEM((grows, dk), F32), pltpu.VMEM((grows, dk), F32),
            pltpu.VMEM((SEQ, dk), F32),
            pltpu.VMEM((n_chunks, dk, 2 * dk), F32), pltpu.VMEM((n_chunks, dk, 2 * dk), BF16),
            pltpu.VMEM((n_chunks, dk), F32), pltpu.VMEM((n_chunks, dk), F32),
            pltpu.VMEM((n_lv + 1, HG_CHUNK, HG_CHUNK), F32)],
        compiler_params=_cparams(("parallel", "parallel")),
        name=name,
    )(pm, pm, pm, pg, pg, lb, norm_w.reshape(1, HG_EXPAND))


def _wattn_kernel(sink_ref, q_ref, kp_ref, kc_ref, kn_ref, vp_ref, vc_ref, vn_ref, g_ref, o_ref,
                  bias_ref, s_ref, p_ref, rden_ref):
    i = pl.program_id(1)
    band = 3 * SW_BLOCK
    pairs = SW_GROUP // 2
    qi = lax.broadcasted_iota(jnp.int32, (SW_BLOCK, band), 0)
    kj = lax.broadcasted_iota(jnp.int32, (SW_BLOCK, band), 1) - SW_BLOCK
    kpos = i * SW_BLOCK + kj
    valid = (jnp.abs(kj - qi) <= SW_WINDOW) & (kpos >= 0) & (kpos < SEQ)
    bias_ref[...] = jnp.where(valid, 0.0, -jnp.inf).astype(F32)
    low = lax.broadcasted_iota(jnp.int32, (band, LANES), 1) < HEAD_DIM
    for c in range(SW_KV_HEADS):
        slot = c % 2
        cs = slice(c * LANES, (c + 1) * LANES)
        k2 = jnp.concatenate([kp_ref[:, cs], kc_ref[:, cs], kn_ref[:, cs]], axis=0)
        v2 = jnp.concatenate([vp_ref[:, cs], vc_ref[:, cs], vn_ref[:, cs]], axis=0)
        zero = jnp.zeros_like(k2)
        kbd = jnp.concatenate([jnp.where(low, k2, zero), jnp.where(low, zero, k2)], axis=0)
        vbd = jnp.concatenate([jnp.where(low, v2, zero), jnp.where(low, zero, v2)], axis=0)
        col0 = c * SW_GROUP * HEAD_DIM
        lhs = jnp.concatenate([q_ref[:, col0 + p * LANES:col0 + (p + 1) * LANES] for p in range(pairs)], axis=0)
        s_ref[slot] = lax.dot_general(lhs, kbd, _NT, preferred_element_type=F32)
        for p in range(pairs):
            for hh in range(2):
                sink = sink_ref[c * SW_GROUP + 2 * p + hh] * math.log2(math.e)
                for st in range(SW_BLOCK // SW_STRIP):
                    rows = pl.ds(p * SW_BLOCK + st * SW_STRIP, SW_STRIP)
                    brow = pl.ds(st * SW_STRIP, SW_STRIP)
                    tiles = [s_ref[slot, rows, hh * band + t * LANES:hh * band + (t + 1) * LANES]
                             + bias_ref[brow, t * LANES:(t + 1) * LANES] for t in range(3)]
                    mx = jnp.maximum(jnp.maximum(tiles[0], tiles[1]), tiles[2])
                    m = jnp.maximum(jnp.max(mx, axis=1, keepdims=True), sink)
                    ps = [jnp.exp2(t - m) for t in tiles]
                    den = jnp.sum(ps[0] + ps[1] + ps[2], axis=1, keepdims=True) + jnp.exp2(sink - m)
                    for t in range(3):
                        p_ref[slot, rows, hh * band + t * LANES:hh * band + (t + 1) * LANES] = ps[t].astype(BF16)
                    rden_ref[slot, rows, hh * HEAD_DIM:(hh + 1) * HEAD_DIM] = jnp.broadcast_to(
                        1.0 / den, (SW_STRIP, HEAD_DIM))
        o = jnp.dot(p_ref[slot], vbd, preferred_element_type=F32) * rden_ref[slot]
        for p in range(pairs):
            col = col0 + p * LANES
            o_ref[:, col:col + LANES] = (o[p * SW_BLOCK:(p + 1) * SW_BLOCK]
                                         * g_ref[:, col:col + LANES].astype(F32)).astype(o_ref.dtype)


def _wattn(po, sink, name):
    nb = SEQ // SW_BLOCK
    kv_w = SW_KV_HEADS * LANES
    kcol = 2 * D_INNER // kv_w
    vcol = kcol + 1
    rows = (SW_GROUP // 2) * SW_BLOCK

    def kv(col, shift):
        return pl.BlockSpec((SW_BLOCK, kv_w), lambda b, i: (b * nb + jnp.clip(i + shift, 0, nb - 1), col))

    return pl.pallas_call(
        _wattn_kernel,
        grid=(BATCH, nb),
        in_specs=[pl.BlockSpec(memory_space=pltpu.SMEM),
                  pl.BlockSpec((SW_BLOCK, D_INNER), lambda b, i: (b * nb + i, 0)),
                  kv(kcol, -1), kv(kcol, 0), kv(kcol, 1),
                  kv(vcol, -1), kv(vcol, 0), kv(vcol, 1),
                  pl.BlockSpec((SW_BLOCK, D_INNER), lambda b, i: (b * nb + i, 1))],
        out_specs=pl.BlockSpec((SW_BLOCK, D_INNER), lambda b, i: (b * nb + i, 0)),
        out_shape=jax.ShapeDtypeStruct((M_TOK, D_INNER), BF16),
        scratch_shapes=[pltpu.VMEM((SW_BLOCK, 3 * SW_BLOCK), F32),
                        pltpu.VMEM((2, rows, 6 * SW_BLOCK), F32),
                        pltpu.VMEM((2, rows, 6 * SW_BLOCK), BF16),
                        pltpu.VMEM((2, rows, LANES), F32)],
        compiler_params=_cparams(("parallel", "arbitrary")),
        name=name,
    )(sink, po, po, po, po, po, po, po, po)


def _rope_tables():
    inv = ROPE_THETA ** (-jnp.arange(0, ROPE_DIM, 2, dtype=F32) / ROPE_DIM)
    ang = jnp.arange(SEQ, dtype=F32)[:, None] * inv[None, :]
    cos, sin = jnp.cos(ang), jnp.sin(ang)
    half = ROPE_DIM // 2
    m = np.arange(LANES) % HEAD_DIM
    idx = jnp.asarray(m % half)
    cos_l, sin_l = cos[:, idx], sin[:, idx]
    c = jnp.where(jnp.asarray(m < ROPE_DIM)[None, :], cos_l, 1.0)
    s1 = jnp.where(jnp.asarray(m < half)[None, :], -sin_l, 0.0)
    s2 = jnp.where(jnp.asarray((m >= half) & (m < ROPE_DIM))[None, :], sin_l, 0.0)
    return jnp.stack([c, s1, s2]).astype(F32)


def _dup_heads(w):
    d, n = w.shape[0], w.shape[1] // HEAD_DIM
    return jnp.repeat(w.reshape(d, n, 1, HEAD_DIM), 2, axis=2).reshape(d, 2 * n * HEAD_DIM)


def kernel(x, norm_even, w_in_even, lambda_qk, subln_w, hgrn_norm_w, hgrn_lb_logits, w_out_even,
           norm_odd, w_in_odd, sink_logits, w_out_odd, final_norm):
    rope_tab = _rope_tables()
    lb_cum = jnp.cumsum(jax.nn.softmax(hgrn_lb_logits.astype(F32), axis=1), axis=1)
    lb_all = lb_cum - lb_cum[:, :1]

    n_tiles = lambda width: width // PROJ_TN
    t1 = n_tiles(DA_WIDTH)
    even_groups = ((0, t1, "rope_q"), (t1, 2 * t1, "rope_k"), (2 * t1, 3 * t1, "plain"),
                   (3 * t1, 5 * t1, "silu"), (5 * t1, 6 * t1, "plain"), (6 * t1, 7 * t1, "silu"))
    n_gate_tiles = 2 * t1
    even_map = lambda j: jnp.where(j < 5 * t1, j, j + n_gate_tiles)
    gate_groups = ((0, n_gate_tiles, "plain"),)
    gate_map = lambda j: j + 5 * t1
    tq, tkv = n_tiles(D_INNER), n_tiles(2 * SW_KV_HEADS * HEAD_DIM)
    odd_groups = ((0, tq, "rope_q"), (tq, 2 * tq, "silu"), (2 * tq, 2 * tq + tkv, "rope_k"),
                  (2 * tq + tkv, 2 * tq + 2 * tkv, "plain"))

    xf = x.reshape(M_TOK, D_MODEL)
    h = _norm(xf, norm_even[0])
    out = None
    for layer in range(DEPTH):
        j = layer // 2
        last = layer == DEPTH - 1
        if last:
            next_norm = final_norm
        elif layer % 2 == 0:
            next_norm = norm_odd[j]
        else:
            next_norm = norm_even[j + 1]
        if layer % 2 == 0:
            lam_init = 0.8 - 0.6 * math.exp(-0.3 * layer)
            w = w_in_even[j].astype(BF16)
            pm = _proj(h, w, rope_tab, even_groups, 7 * DA_WIDTH, BF16, even_map, f"proj_even{j}")
            pg = _proj(h, w, rope_tab, gate_groups, 2 * HG_WIDTH, F32, gate_map, f"proj_gate{j}")
            oa = _dattn(pm, lambda_qk[j], subln_w[j], lam_init, f"diff_attn{j}")
            ob = _hgrn(pm, pg, lb_all[:, j], hgrn_norm_w[j], f"hgrn{j}")
            res = _outproj([oa, ob], w_out_even[j].astype(BF16), xf, next_norm, last, f"outproj_even{j}")
        else:
            wi = w_in_odd[j]
            kv_w = SW_KV_HEADS * HEAD_DIM
            w = jnp.concatenate([wi[:, :D_INNER], wi[:, D_INNER + 2 * kv_w:],
                                 _dup_heads(wi[:, D_INNER:D_INNER + kv_w]),
                                 _dup_heads(wi[:, D_INNER + kv_w:D_INNER + 2 * kv_w])], axis=1).astype(BF16)
            po = _proj(h, w, rope_tab, odd_groups, w.shape[1], BF16, lambda jj: jj, f"proj_odd{j}")
            oc = _wattn(po, sink_logits[j], f"win_attn{j}")
            res = _outproj([oc], w_out_odd[j].astype(BF16), xf, next_norm, last, f"outproj_odd{j}")
        if last:
            out = res[0]
        else:
            xf, h = res
    return out.reshape(BATCH, SEQ, D_MODEL)
```

```python
import functools
import math

import jax
import jax.numpy as jnp
import numpy as np
from jax import lax
from jax.experimental import pallas as pl
from jax.experimental.pallas import tpu as pltpu

D_MODEL = 1024
BATCH = 4
SEQ = 4096
DEPTH = 4
D_INNER = 2 * D_MODEL
HEAD_DIM = 64
ROPE_THETA = 500000.0
ROPE_DIM = HEAD_DIM // 4
EPS = 1e-6
DA_WIDTH = D_INNER // 2
DA_VDIM = 2 * HEAD_DIM
DA_HEADS = DA_WIDTH // DA_VDIM
HG_WIDTH = D_INNER - DA_WIDTH
HG_EXPAND = 128
HG_HEADS = HG_WIDTH // HG_EXPAND
SW_HEADS = D_INNER // HEAD_DIM
SW_KV_HEADS = SW_HEADS // 8
SW_GROUP = SW_HEADS // SW_KV_HEADS
SW_WINDOW = 128
SW_BLOCK = 128
SW_STRIP = 32

M_TOK = BATCH * SEQ
LANES = 128
F32 = jnp.float32
BF16 = jnp.bfloat16
NEG_BIG = -1e30

VMEM_LIMIT = 48 * 1024 * 1024

NORM_TM = 512
PROJ_TM = 1024
PROJ_TN = 1024
PROJ_RB = 256
OUT_TM = 512
DA_TQ = 256
DA_SUBS = (1024, 1024, 1024, 1024)
DA_RC = 128
HG_CHUNK = 128
HG_GROUP = 8
HG_PITCH = HG_CHUNK + 8
LOG2E = math.log2(math.e)

Q_SCALE = HEAD_DIM ** -0.5 * math.log2(math.e)

_NT = (((1,), (1,)), ((), ()))
_TN = (((0,), (0,)), ((), ()))


def _cparams(sem):
    return pltpu.CompilerParams(dimension_semantics=sem, vmem_limit_bytes=VMEM_LIMIT)


def _rms(x, w):
    return x * lax.rsqrt(jnp.mean(x * x, axis=-1, keepdims=True) + EPS) * w


def _norm_kernel(x_ref, w_ref, h_ref):
    h_ref[...] = _rms(x_ref[...], w_ref[...]).astype(h_ref.dtype)


def _norm(x, w):
    return pl.pallas_call(
        _norm_kernel,
        grid=(M_TOK // NORM_TM,),
        in_specs=[pl.BlockSpec((NORM_TM, D_MODEL), lambda i: (i, 0)),
                  pl.BlockSpec((1, D_MODEL), lambda i: (0, 0))],
        out_specs=pl.BlockSpec((NORM_TM, D_MODEL), lambda i: (i, 0)),
        out_shape=jax.ShapeDtypeStruct((M_TOK, D_MODEL), BF16),
        compiler_params=_cparams(("parallel",)),
        name="rmsnorm0",
    )(x, w.reshape(1, D_MODEL))


def _rope_cols(acc, rope_ref, scale):
    c, s1, s2 = rope_ref
    outs = []
    for t in range(acc.shape[1] // LANES):
        xs = acc[:, t * LANES:(t + 1) * LANES]
        y = xs * c + pltpu.roll(xs, LANES - ROPE_DIM // 2, 1) * s1 + pltpu.roll(xs, ROPE_DIM // 2, 1) * s2
        outs.append(y * scale if scale != 1.0 else y)
    return jnp.concatenate(outs, axis=1)


def _silu(x):
    return x * (1.0 / (1.0 + jnp.exp(-x)))


def _proj_kernel(h_ref, w_ref, rope_ref, o_ref, *, groups):
    j = pl.program_id(1)
    for lo, hi, mode in groups:
        @pl.when((j >= lo) & (j < hi))
        def _(mode=mode):
            for r in range(PROJ_TM // PROJ_RB):
                rows = pl.ds(r * PROJ_RB, PROJ_RB)
                acc = jnp.dot(h_ref[rows, :], w_ref[...], preferred_element_type=F32)
                if mode in ("rope_q", "rope_k", "rope_k|plain"):
                    tabs = (rope_ref[0, rows, :], rope_ref[1, rows, :], rope_ref[2, rows, :])
                    if mode == "rope_k|plain":
                        half = acc.shape[1] // 2
                        y = jnp.concatenate([_rope_cols(acc[:, :half], tabs, 1.0), acc[:, half:]], axis=1)
                    else:
                        y = _rope_cols(acc, tabs, Q_SCALE if mode == "rope_q" else 1.0)
                elif mode == "silu":
                    y = _silu(acc)
                else:
                    y = acc
                o_ref[rows, :] = y.astype(o_ref.dtype)


def _proj(h, w, rope_tab, groups, n_out, out_dtype, col_map, name):
    tn = PROJ_TN
    n_j = n_out // tn
    s_blocks = SEQ // PROJ_TM
    return pl.pallas_call(
        functools.partial(_proj_kernel, groups=groups),
        grid=(M_TOK // PROJ_TM, n_j),
        in_specs=[pl.BlockSpec((PROJ_TM, D_MODEL), lambda i, j: (i, 0)),
                  pl.BlockSpec((D_MODEL, tn), lambda i, j: (0, col_map(j))),
                  pl.BlockSpec((3, PROJ_TM, LANES), lambda i, j: (0, i % s_blocks, 0))],
        out_specs=pl.BlockSpec((PROJ_TM, tn), lambda i, j: (i, j)),
        out_shape=jax.ShapeDtypeStruct((M_TOK, n_out), out_dtype),
        compiler_params=_cparams(("parallel", "arbitrary")),
        name=name,
    )(h, w, rope_tab)


def _outproj_kernel(*refs, n_in, final):
    o_refs = refs[:n_in]
    w_ref, x_ref, nw_ref = refs[n_in:n_in + 3]
    outs = refs[n_in + 3:]
    acc = x_ref[...]
    k0 = 0
    for o_ref in o_refs:
        kw = o_ref.shape[1]
        acc = acc + jnp.dot(o_ref[...], w_ref[k0:k0 + kw, :], preferred_element_type=F32)
        k0 += kw
    y = _rms(acc, nw_ref[...])
    if final:
        outs[0][...] = y
    else:
        outs[0][...] = acc
        outs[1][...] = y.astype(BF16)


def _outproj(o_list, w, x, next_norm_w, final, name):
    n_in = len(o_list)
    row = lambda i: (i, 0)
    in_specs = [pl.BlockSpec((OUT_TM, o.shape[1]), row) for o in o_list]
    in_specs += [pl.BlockSpec((D_INNER, D_MODEL), lambda i: (0, 0)),
                 pl.BlockSpec((OUT_TM, D_MODEL), row),
                 pl.BlockSpec((1, D_MODEL), lambda i: (0, 0))]
    if final:
        out_shape = [jax.ShapeDtypeStruct((M_TOK, D_MODEL), F32)]
        out_specs = [pl.BlockSpec((OUT_TM, D_MODEL), row)]
    else:
        out_shape = [jax.ShapeDtypeStruct((M_TOK, D_MODEL), F32),
                     jax.ShapeDtypeStruct((M_TOK, D_MODEL), BF16)]
        out_specs = [pl.BlockSpec((OUT_TM, D_MODEL), row), pl.BlockSpec((OUT_TM, D_MODEL), row)]
    return pl.pallas_call(
        functools.partial(_outproj_kernel, n_in=n_in, final=final),
        grid=(M_TOK // OUT_TM,),
        in_specs=in_specs, out_specs=out_specs, out_shape=out_shape,
        compiler_params=_cparams(("parallel",)),
        name=name,
    )(*o_list, w, x, next_norm_w.reshape(1, D_MODEL))


def _dattn_step(lam_ref, sw_ref, q_ref, k_ref, v_ref, g_ref, o_ref, qm_ref, mx_ref, ls_ref, a_ref, ot_ref, l1_ref,
                s_in, m_in, s_out, m_out, lam_init):
    bounds = np.cumsum((0,) + DA_SUBS)
    assert bounds[-1] == SEQ
    sub8 = (DA_RC // 8, 8, DA_TQ)

    ot = ot_ref[...] / l1_ref[0:1, :]
    yt = ot * lax.rsqrt(jnp.mean(ot * ot, axis=0, keepdims=True) + EPS) * sw_ref[...] * (1.0 - lam_init)
    o_ref[...] = (yt.T * g_ref[...].astype(F32)).astype(o_ref.dtype)

    q = q_ref[...]
    lane = lax.broadcasted_iota(jnp.int32, q.shape, 1)
    qm_ref[0] = jnp.where(lane < HEAD_DIM, q, jnp.zeros_like(q))
    qm_ref[1] = jnp.where(lane >= HEAD_DIM, q, jnp.zeros_like(q))
    mx_ref[...] = jnp.full(mx_ref.shape, NEG_BIG, F32)
    ls_ref[...] = jnp.zeros(ls_ref.shape, F32)

    for lo, hi in zip(bounds[:-1], bounds[1:]):
        lo, hi = int(lo), int(hi)
        kb = k_ref[lo:hi, :]
        for mp in range(2):
            st = lax.dot_general(kb, qm_ref[mp], _NT, preferred_element_type=F32)
            s_in[mp, lo:hi, :] = st
            mx_ref[mp] = jnp.maximum(mx_ref[mp], jnp.max(st.reshape((hi - lo) // 8, 8, DA_TQ), axis=0))
        for mp in range(2):
            m = m_out[mp]
            part = ls_ref[mp]
            for r0 in range(lo, hi, DA_RC):
                rows = slice(r0, r0 + DA_RC)
                p = jnp.exp2(s_out[mp, rows, :].reshape(sub8) - m[None])
                s_out[mp, rows, :] = p.reshape(DA_RC, DA_TQ)
                part = part + jnp.sum(p, axis=0)
            ls_ref[mp] = part

    lp = lam_ref[...]
    lam = (jnp.exp(jnp.sum(lp[0:1] * lp[1:2], axis=1, keepdims=True))
           - jnp.exp(jnp.sum(lp[2:3] * lp[3:4], axis=1, keepdims=True)) + lam_init)
    l1 = jnp.sum(ls_ref[0], axis=0, keepdims=True)
    l2 = jnp.sum(ls_ref[1], axis=0, keepdims=True)
    rho = jnp.broadcast_to(lam * l1 / l2, (8, DA_TQ))
    for r0 in range(0, SEQ, DA_RC):
        rows = slice(r0, r0 + DA_RC)
        a = s_out[0, rows, :].reshape(sub8) - s_out[1, rows, :].reshape(sub8) * rho[None]
        a_ref[rows, :] = a.reshape(DA_RC, DA_TQ).astype(BF16)
    ot_ref[...] = lax.dot_general(v_ref[...], a_ref[...], _TN, preferred_element_type=F32)
    l1_ref[...] = jnp.broadcast_to(l1, (8, DA_TQ))
    for mp in range(2):
        m_in[mp] = jnp.broadcast_to(jnp.max(mx_ref[mp], axis=0, keepdims=True), (8, DA_TQ))


def _dattn_kernel(lam_ref, sw_ref, q_ref, k_ref, v_ref, g_ref, o_ref,
                  qm_ref, s0_ref, s1_ref, m0_ref, m1_ref, mx_ref, ls_ref, a_ref, ot_ref, l1_ref, *, lam_init):
    g = pl.program_id(0)
    common = (lam_ref, sw_ref, q_ref, k_ref, v_ref, g_ref, o_ref, qm_ref, mx_ref, ls_ref, a_ref, ot_ref, l1_ref)

    @pl.when(g == 0)
    def _():
        s1_ref[...] = jnp.zeros(s1_ref.shape, F32)
        m1_ref[...] = jnp.zeros(m1_ref.shape, F32)
        ot_ref[...] = jnp.zeros(ot_ref.shape, F32)
        l1_ref[...] = jnp.ones(l1_ref.shape, F32)

    @pl.when(g % 2 == 0)
    def _():
        _dattn_step(*common, s0_ref, m0_ref, s1_ref, m1_ref, lam_init)

    @pl.when(g % 2 == 1)
    def _():
        _dattn_step(*common, s1_ref, m1_ref, s0_ref, m0_ref, lam_init)


def _dattn(pm, lam_p, subln_w, lam_init, name):
    nq = SEQ // DA_TQ
    hb = DA_WIDTH // LANES
    n_tiles = BATCH * DA_HEADS * nq

    def tile_in(g):
        return jnp.minimum(g, n_tiles - 1)

    def tile_out(g):
        return jnp.clip(g - 1, 0, n_tiles - 1)

    def tile_fin(g):
        return jnp.clip(g - 2, 0, n_tiles - 1)

    def bhi(t):
        return t // (DA_HEADS * nq), (t // nq) % DA_HEADS, t % nq

    def q_map(g):
        b, h, i = bhi(tile_in(g))
        return b * nq + i, h

    def k_map(g):
        b, h, _ = bhi(tile_in(g))
        return b, hb + h

    def v_map(g):
        b, h, _ = bhi(tile_out(g))
        return b, 2 * hb + h

    def g_map(g):
        b, h, i = bhi(tile_fin(g))
        return b * nq + i, 3 * hb + h

    def o_map(g):
        b, h, i = bhi(tile_fin(g))
        return b * nq + i, h

    return pl.pallas_call(
        functools.partial(_dattn_kernel, lam_init=lam_init),
        grid=(n_tiles + 2,),
        in_specs=[pl.BlockSpec((4, HEAD_DIM), lambda g: (0, 0)),
                  pl.BlockSpec((DA_VDIM, 1), lambda g: (0, 0)),
                  pl.BlockSpec((DA_TQ, LANES), q_map),
                  pl.BlockSpec((SEQ, LANES), k_map),
                  pl.BlockSpec((SEQ, LANES), v_map),
                  pl.BlockSpec((DA_TQ, LANES), g_map)],
        out_specs=pl.BlockSpec((DA_TQ, LANES), o_map),
        out_shape=jax.ShapeDtypeStruct((M_TOK, DA_WIDTH), BF16),
        scratch_shapes=[pltpu.VMEM((2, DA_TQ, LANES), BF16),
                        pltpu.VMEM((2, SEQ, DA_TQ), F32),
                        pltpu.VMEM((2, SEQ, DA_TQ), F32),
                        pltpu.VMEM((2, 8, DA_TQ), F32),
                        pltpu.VMEM((2, 8, DA_TQ), F32),
                        pltpu.VMEM((2, 8, DA_TQ), F32),
                        pltpu.VMEM((2, 8, DA_TQ), F32),
                        pltpu.VMEM((SEQ, DA_TQ), BF16),
                        pltpu.VMEM((DA_VDIM, DA_TQ), F32),
                        pltpu.VMEM((8, DA_TQ), F32)],
        compiler_params=_cparams(("arbitrary",)),
        name=name,
    )(lam_p, subln_w.reshape(DA_VDIM, 1), pm, pm, pm, pm)


def _hgrn_gates(z, lb, one_m_lb):
    e = jnp.exp2(jnp.abs(z) * (-LOG2E))
    r = 1.0 / (1.0 + e)
    er = e * r
    pos = z >= 0.0
    f = lb + one_m_lb * jnp.where(pos, r, er)
    log2_f = jnp.where(f > 0.0, jnp.log2(f), z * LOG2E)
    key = one_m_lb * jnp.where(pos, er, r)
    return log2_f, key


def _hgrn_kernel(q_ref, v_ref, g_ref, zf_ref, zb_ref, lb_ref, nw_ref, o_ref,
                 zfp, zbp, qp, lq, lk, ks, a_ref, qgf, qgb, kdf, kdb, oi, ut, st,
                 decf, decb, mask):
    C, G, P = HG_CHUNK, HG_GROUP, HG_PITCH
    n_chunks = SEQ // C
    levels = [2 ** i for i in range(1, C.bit_length())]
    n_lv = len(levels)
    lb = lb_ref[...]
    one_m_lb = 1.0 - lb

    arow = lax.broadcasted_iota(jnp.int32, (C, C), 0)
    acol = lax.broadcasted_iota(jnp.int32, (C, C), 1)
    for li, n in enumerate(levels):
        same = (arow // n) == (acol // n)
        cross = ((arow % n) >= n // 2) != ((acol % n) >= n // 2)
        mask[li] = jnp.where(same & cross, 1.0, 0.0).astype(F32)
    mask[n_lv] = jnp.where(arow == acol, 1.0, 0.0).astype(F32)

    def group_body(grp, _):
        r0 = pl.multiple_of(grp * (G * C), G * C)
        for j in range(G):
            src = pl.ds(r0 + j * C, C)
            dst = slice(j * P, j * P + C)
            zfp[dst, :] = zf_ref[src, :]
            zbp[dst, :] = zb_ref[src, :]
            qp[dst, :] = q_ref[src, :].astype(F32)
        pf, pb, kf, kb, qi = [], [], [], [], []
        for t in range(C):
            sl = pl.ds(t, G, stride=P)
            lf, key_f = _hgrn_gates(zfp[sl, :], lb[0:1], one_m_lb[0:1])
            lbk, key_b = _hgrn_gates(zbp[sl, :], lb[1:2], one_m_lb[1:2])
            pf.append(lf)
            pb.append(lbk)
            kf.append(key_f)
            kb.append(key_b)
            qi.append(qp[sl, :])
            ks[sl, :] = key_f + key_b
        for j in range(G):
            rows = slice(j * P, j * P + C)
            a_ref[j] = mask[n_lv] * jnp.sum(qp[rows, :] * ks[rows, :], axis=1, keepdims=True)
        for li, n in enumerate(levels):
            half = n // 2
            for b0 in range(0, C, n):
                mid = b0 + half
                bf = pf[mid - 1]
                bb = pb[mid]
                for t in range(b0, mid):
                    sl = pl.ds(t, G, stride=P)
                    pbt = pb[t]
                    lk[li, sl, :] = kf[t] * jnp.exp2(bf - pf[t])
                    lq[li, sl, :] = qi[t] * jnp.exp2(pbt)
                    pb[t] = pbt + bb
                for t in range(mid, b0 + n):
                    sl = pl.ds(t, G, stride=P)
                    pft = pf[t]
                    lq[li, sl, :] = qi[t] * jnp.exp2(pft)
                    lk[li, sl, :] = kb[t] * jnp.exp2(bb - pb[t])
                    pf[t] = pft + bf
            for j in range(G):
                rows = slice(j * P, j * P + C)
                an = lax.dot_general(lq[li, rows, :].astype(BF16), lk[li, rows, :].astype(BF16), _NT,
                                     preferred_element_type=F32)
                a_ref[j] += mask[li] * an
        pf_last = pf[C - 1]
        pb_first = pb[0]
        decf[pl.ds(pl.multiple_of(grp * G, G), G), :] = jnp.exp2(pf_last)
        decb[pl.ds(pl.multiple_of(grp * G, G), G), :] = jnp.exp2(pb_first)

        for t in range(C):
            sl = pl.ds(t, G, stride=P)
            pft = pf[t]
            pbt = pb[t]
            qt = qi[t]
            qgf[grp, sl, :] = qt * jnp.exp2(pft)
            qgb[grp, sl, :] = qt * jnp.exp2(pbt)
            kdf[sl, :] = kf[t] * jnp.exp2(pf_last - pft)
            kdb[sl, :] = kb[t] * jnp.exp2(pb_first - pbt)
            if t % (C // G) == 0:
                j = t // (C // G)
                c0 = pl.multiple_of((grp * G + j) * C, C)
                oi[pl.ds(c0, C), :] = jnp.dot(a_ref[j].astype(BF16), v_ref[pl.ds(c0, C), :],
                                              preferred_element_type=F32)
        for j in range(G):
            n = grp * G + j
            rows = slice(j * P, j * P + C)
            kd = jnp.concatenate([kdf[rows, :], kdb[rows, :]], axis=1).astype(BF16)
            ut[n] = lax.dot_general(v_ref[pl.ds(pl.multiple_of(n * C, C), C), :], kd, _TN,
                                    preferred_element_type=F32)
        return 0

    lax.fori_loop(0, n_chunks // G, group_body, 0)

    def fwd_scan(n, cur):
        st[n, :, 0:HG_EXPAND] = cur.astype(BF16)
        return decf[pl.ds(n, 1), :] * cur + ut[n, :, 0:HG_EXPAND]

    lax.fori_loop(0, n_chunks, fwd_scan, jnp.zeros((HG_EXPAND, HG_EXPAND), F32))

    def bwd_scan(i, cur):
        n = n_chunks - 1 - i
        st[n, :, HG_EXPAND:2 * HG_EXPAND] = cur.astype(BF16)
        return decb[pl.ds(n, 1), :] * cur + ut[n, :, HG_EXPAND:2 * HG_EXPAND]

    lax.fori_loop(0, n_chunks, bwd_scan, jnp.zeros((HG_EXPAND, HG_EXPAND), F32))

    def out_group(grp, _):
        for j in range(G):
            n = grp * G + j
            rows = slice(j * P, j * P + C)
            c0 = pl.multiple_of(n * C, C)
            qg = jnp.concatenate([qgf[grp, rows, :], qgb[grp, rows, :]], axis=1).astype(BF16)
            o = oi[pl.ds(c0, C), :] + lax.dot_general(qg, st[n], _NT, preferred_element_type=F32)
            y = _rms(o, nw_ref[...]) * g_ref[pl.ds(c0, C), :].astype(F32)
            o_ref[pl.ds(c0, C), :] = y.astype(o_ref.dtype)
        return 0

    lax.fori_loop(0, n_chunks // G, out_group, 0)


def _hgrn(pm, pg, lb, norm_w, name):
    hb = HG_WIDTH // LANES
    dk = HG_EXPAND
    n_chunks = SEQ // HG_CHUNK
    n_groups = n_chunks // HG_GROUP
    n_lv = HG_CHUNK.bit_length() - 1
    grows = HG_GROUP * HG_PITCH
    seq_blk = lambda off: pl.BlockSpec((SEQ, LANES), lambda b, h: (b, off + h))
    return pl.pallas_call(
        _hgrn_kernel,
        grid=(BATCH, HG_HEADS),
        in_specs=[seq_blk(4 * hb), seq_blk(5 * hb), seq_blk(6 * hb), seq_blk(0), seq_blk(hb),
                  pl.BlockSpec((2, LANES), lambda b, h: (0, h)),
                  pl.BlockSpec((1, HG_EXPAND), lambda b, h: (0, 0))],
        out_specs=pl.BlockSpec((SEQ, LANES), lambda b, h: (b, h)),
        out_shape=jax.ShapeDtypeStruct((M_TOK, HG_WIDTH), BF16),
        scratch_shapes=[
            pltpu.VMEM((grows, dk), F32), pltpu.VMEM((grows, dk), F32), pltpu.VMEM((grows, dk), F32),
            pltpu.VMEM((n_lv, grows, dk), F32), pltpu.VMEM((n_lv, grows, dk), F32),
            pltpu.VMEM((grows, dk), F32),
            pltpu.VMEM((HG_GROUP, HG_CHUNK, HG_CHUNK), F32),
            pltpu.VMEM((n_groups, grows, dk), F32), pltpu.VMEM((n_groups, grows, dk), F32),
            pltpu.VMEM((grows, dk), F32), pltpu.VMEM((grows, dk), F32),
            pltpu.VMEM((SEQ, dk), F32),
            pltpu.VMEM((n_chunks, dk, 2 * dk), F32), pltpu.VMEM((n_chunks, dk, 2 * dk), BF16),
            pltpu.VMEM((n_chunks, dk), F32), pltpu.VMEM((n_chunks, dk), F32),
            pltpu.VMEM((n_lv + 1, HG_CHUNK, HG_CHUNK), F32)],
        compiler_params=_cparams(("parallel", "parallel")),
        name=name,
    )(pm, pm, pm, pg, pg, lb, norm_w.reshape(1, HG_EXPAND))


def _wattn_kernel(sink_ref, q_ref, kp_ref, kc_ref, kn_ref, vp_ref, vc_ref, vn_ref, g_ref, o_ref,
                  bias_ref, s_ref, p_ref, rden_ref):
    i = pl.program_id(1)
    band = 3 * SW_BLOCK
    pairs = SW_GROUP // 2
    qi = lax.broadcasted_iota(jnp.int32, (SW_BLOCK, band), 0)
    kj = lax.broadcasted_iota(jnp.int32, (SW_BLOCK, band), 1) - SW_BLOCK
    kpos = i * SW_BLOCK + kj
    valid = (jnp.abs(kj - qi) <= SW_WINDOW) & (kpos >= 0) & (kpos < SEQ)
    bias_ref[...] = jnp.where(valid, 0.0, -jnp.inf).astype(F32)
    low = lax.broadcasted_iota(jnp.int32, (band, LANES), 1) < HEAD_DIM
    for c in range(SW_KV_HEADS):
        slot = c % 2
        cs = slice(c * LANES, (c + 1) * LANES)
        k2 = jnp.concatenate([kp_ref[:, cs], kc_ref[:, cs], kn_ref[:, cs]], axis=0)
        v2 = jnp.concatenate([vp_ref[:, cs], vc_ref[:, cs], vn_ref[:, cs]], axis=0)
        zero = jnp.zeros_like(k2)
        kbd = jnp.concatenate([jnp.where(low, k2, zero), jnp.where(low, zero, k2)], axis=0)
        vbd = jnp.concatenate([jnp.where(low, v2, zero), jnp.where(low, zero, v2)], axis=0)
        col0 = c * SW_GROUP * HEAD_DIM
        lhs = jnp.concatenate([q_ref[:, col0 + p * LANES:col0 + (p + 1) * LANES] for p in range(pairs)], axis=0)
        s_ref[slot] = lax.dot_general(lhs, kbd, _NT, preferred_element_type=F32)
        for p in range(pairs):
            for hh in range(2):
                sink = sink_ref[c * SW_GROUP + 2 * p + hh] * math.log2(math.e)
                for st in range(SW_BLOCK // SW_STRIP):
                    rows = pl.ds(p * SW_BLOCK + st * SW_STRIP, SW_STRIP)
                    brow = pl.ds(st * SW_STRIP, SW_STRIP)
                    tiles = [s_ref[slot, rows, hh * band + t * LANES:hh * band + (t + 1) * LANES]
                             + bias_ref[brow, t * LANES:(t + 1) * LANES] for t in range(3)]
                    mx = jnp.maximum(jnp.maximum(tiles[0], tiles[1]), tiles[2])
                    m = jnp.maximum(jnp.max(mx, axis=1, keepdims=True), sink)
                    ps = [jnp.exp2(t - m) for t in tiles]
                    den = jnp.sum(ps[0] + ps[1] + ps[2], axis=1, keepdims=True) + jnp.exp2(sink - m)
                    for t in range(3):
                        p_ref[slot, rows, hh * band + t * LANES:hh * band + (t + 1) * LANES] = ps[t].astype(BF16)
                    rden_ref[slot, rows, hh * HEAD_DIM:(hh + 1) * HEAD_DIM] = jnp.broadcast_to(
                        1.0 / den, (SW_STRIP, HEAD_DIM))
        o = jnp.dot(p_ref[slot], vbd, preferred_element_type=F32) * rden_ref[slot]
        for p in range(pairs):
            col = col0 + p * LANES
            o_ref[:, col:col + LANES] = (o[p * SW_BLOCK:(p + 1) * SW_BLOCK]
                                         * g_ref[:, col:col + LANES].astype(F32)).astype(o_ref.dtype)


def _wattn(po, sink, name):
    nb = SEQ // SW_BLOCK
    kv_w = SW_KV_HEADS * LANES
    kcol = 2 * D_INNER // kv_w
    vcol = kcol + 1
    rows = (SW_GROUP // 2) * SW_BLOCK

    def kv(col, shift):
        return pl.BlockSpec((SW_BLOCK, kv_w), lambda b, i: (b * nb + jnp.clip(i + shift, 0, nb - 1), col))

    return pl.pallas_call(
        _wattn_kernel,
        grid=(BATCH, nb),
        in_specs=[pl.BlockSpec(memory_space=pltpu.SMEM),
                  pl.BlockSpec((SW_BLOCK, D_INNER), lambda b, i: (b * nb + i, 0)),
                  kv(kcol, -1), kv(kcol, 0), kv(kcol, 1),
                  kv(vcol, -1), kv(vcol, 0), kv(vcol, 1),
                  pl.BlockSpec((SW_BLOCK, D_INNER), lambda b, i: (b * nb + i, 1))],
        out_specs=pl.BlockSpec((SW_BLOCK, D_INNER), lambda b, i: (b * nb + i, 0)),
        out_shape=jax.ShapeDtypeStruct((M_TOK, D_INNER), BF16),
        scratch_shapes=[pltpu.VMEM((SW_BLOCK, 3 * SW_BLOCK), F32),
                        pltpu.VMEM((2, rows, 6 * SW_BLOCK), F32),
                        pltpu.VMEM((2, rows, 6 * SW_BLOCK), BF16),
                        pltpu.VMEM((2, rows, LANES), F32)],
        compiler_params=_cparams(("parallel", "arbitrary")),
        name=name,
    )(sink, po, po, po, po, po, po, po, po)


def _rope_tables():
    inv = ROPE_THETA ** (-jnp.arange(0, ROPE_DIM, 2, dtype=F32) / ROPE_DIM)
    ang = jnp.arange(SEQ, dtype=F32)[:, None] * inv[None, :]
    cos, sin = jnp.cos(ang), jnp.sin(ang)
    half = ROPE_DIM // 2
    m = np.arange(LANES) % HEAD_DIM
    idx = jnp.asarray(m % half)
    cos_l, sin_l = cos[:, idx], sin[:, idx]
    c = jnp.where(jnp.asarray(m < ROPE_DIM)[None, :], cos_l, 1.0)
    s1 = jnp.where(jnp.asarray(m < half)[None, :], -sin_l, 0.0)
    s2 = jnp.where(jnp.asarray((m >= half) & (m < ROPE_DIM))[None, :], sin_l, 0.0)
    return jnp.stack([c, s1, s2]).astype(F32)


def _dup_heads(w):
    d, n = w.shape[0], w.shape[1] // HEAD_DIM
    return jnp.repeat(w.reshape(d, n, 1, HEAD_DIM), 2, axis=2).reshape(d, 2 * n * HEAD_DIM)


def kernel(x, norm_even, w_in_even, lambda_qk, subln_w, hgrn_norm_w, hgrn_lb_logits, w_out_even,
           norm_odd, w_in_odd, sink_logits, w_out_odd, final_norm):
    rope_tab = _rope_tables()
    lb_cum = jnp.cumsum(jax.nn.softmax(hgrn_lb_logits.astype(F32), axis=1), axis=1)
    lb_all = lb_cum - lb_cum[:, :1]

    t1 = DA_WIDTH // PROJ_TN
    even_groups = ((0, t1, "rope_q"), (t1, 2 * t1, "rope_k"), (2 * t1, 3 * t1, "plain"),
                   (3 * t1, 5 * t1, "silu"), (5 * t1, 6 * t1, "plain"), (6 * t1, 7 * t1, "silu"))
    n_gate_tiles = 2 * t1
    even_map = lambda j: jnp.where(j < 5 * t1, j, j + n_gate_tiles)
    gate_groups = ((0, n_gate_tiles, "plain"),)
    gate_map = lambda j: j + 5 * t1
    tq = D_INNER // PROJ_TN
    assert 2 * (2 * SW_KV_HEADS * HEAD_DIM) == PROJ_TN
    odd_groups = ((0, tq, "rope_q"), (tq, 2 * tq, "silu"), (2 * tq, 2 * tq + 1, "rope_k|plain"))

    xf = x.reshape(M_TOK, D_MODEL)
    h = _norm(xf, norm_even[0])
    out = None
    for layer in range(DEPTH):
        j = layer // 2
        last = layer == DEPTH - 1
        if last:
            next_norm = final_norm
        elif layer % 2 == 0:
            next_norm = norm_odd[j]
        else:
            next_norm = norm_even[j + 1]
        if layer % 2 == 0:
            lam_init = 0.8 - 0.6 * math.exp(-0.3 * layer)
            w = w_in_even[j].astype(BF16)
            pm = _proj(h, w, rope_tab, even_groups, 7 * DA_WIDTH, BF16, even_map, f"proj_even{j}")
            pg = _proj(h, w, rope_tab, gate_groups, 2 * HG_WIDTH, F32, gate_map, f"proj_gate{j}")
            oa = _dattn(pm, lambda_qk[j], subln_w[j], lam_init, f"diff_attn{j}")
            ob = _hgrn(pm, pg, lb_all[:, j], hgrn_norm_w[j], f"hgrn{j}")
            res = _outproj([oa, ob], w_out_even[j].astype(BF16), xf, next_norm, last, f"outproj_even{j}")
        else:
            wi = w_in_odd[j]
            kv_w = SW_KV_HEADS * HEAD_DIM
            w = jnp.concatenate([wi[:, :D_INNER], wi[:, D_INNER + 2 * kv_w:],
                                 _dup_heads(wi[:, D_INNER:D_INNER + kv_w]),
                                 _dup_heads(wi[:, D_INNER + kv_w:D_INNER + 2 * kv_w])], axis=1).astype(BF16)
            po = _proj(h, w, rope_tab, odd_groups, w.shape[1], BF16, lambda jj: jj, f"proj_odd{j}")
            oc = _wattn(po, sink_logits[j], f"win_attn{j}")
            res = _outproj([oc], w_out_odd[j].astype(BF16), xf, next_norm, last, f"outproj_odd{j}")
        if last:
            out = res[0]
        else:
            xf, h = res
    return out.reshape(BATCH, SEQ, D_MODEL)
```

```python
import functools
import math

import jax
import jax.numpy as jnp
import numpy as np
from jax import lax
from jax.experimental import pallas as pl
from jax.experimental.pallas import tpu as pltpu

D_MODEL = 1024
BATCH = 4
SEQ = 4096
DEPTH = 4
D_INNER = 2 * D_MODEL
HEAD_DIM = 64
ROPE_THETA = 500000.0
ROPE_DIM = HEAD_DIM // 4
EPS = 1e-6
DA_WIDTH = D_INNER // 2
DA_VDIM = 2 * HEAD_DIM
DA_HEADS = DA_WIDTH // DA_VDIM
HG_WIDTH = D_INNER - DA_WIDTH
HG_EXPAND = 128
HG_HEADS = HG_WIDTH // HG_EXPAND
SW_HEADS = D_INNER // HEAD_DIM
SW_KV_HEADS = SW_HEADS // 8
SW_GROUP = SW_HEADS // SW_KV_HEADS
SW_WINDOW = 128
SW_BLOCK = 128
SW_STRIP = 32

M_TOK = BATCH * SEQ
LANES = 128
F32 = jnp.float32
BF16 = jnp.bfloat16
NEG_BIG = -1e30

VMEM_LIMIT = 48 * 1024 * 1024

NORM_TM = 512
PROJ_TM = 1024
PROJ_TN = 1024
PROJ_RB = 256
OUT_TM = 512
DA_TQ = 256
DA_SUBS = (1024, 1024, 1024, 1024)
DA_RC = 128
HG_CHUNK = 128
HG_GROUP = 8
HG_PITCH = HG_CHUNK + 8
LOG2E = math.log2(math.e)

Q_SCALE = HEAD_DIM ** -0.5 * math.log2(math.e)

_NT = (((1,), (1,)), ((), ()))
_TN = (((0,), (0,)), ((), ()))


def _cparams(sem):
    return pltpu.CompilerParams(dimension_semantics=sem, vmem_limit_bytes=VMEM_LIMIT)


def _rms(x, w):
    return x * lax.rsqrt(jnp.mean(x * x, axis=-1, keepdims=True) + EPS) * w


def _norm_kernel(x_ref, w_ref, h_ref):
    h_ref[...] = _rms(x_ref[...], w_ref[...]).astype(h_ref.dtype)


def _norm(x, w):
    return pl.pallas_call(
        _norm_kernel,
        grid=(M_TOK // NORM_TM,),
        in_specs=[pl.BlockSpec((NORM_TM, D_MODEL), lambda i: (i, 0)),
                  pl.BlockSpec((1, D_MODEL), lambda i: (0, 0))],
        out_specs=pl.BlockSpec((NORM_TM, D_MODEL), lambda i: (i, 0)),
        out_shape=jax.ShapeDtypeStruct((M_TOK, D_MODEL), BF16),
        compiler_params=_cparams(("parallel",)),
        name="rmsnorm0",
    )(x, w.reshape(1, D_MODEL))


def _rope_cols(acc, rope_ref, scale):
    c, s1, s2 = rope_ref
    outs = []
    for t in range(acc.shape[1] // LANES):
        xs = acc[:, t * LANES:(t + 1) * LANES]
        y = xs * c + pltpu.roll(xs, LANES - ROPE_DIM // 2, 1) * s1 + pltpu.roll(xs, ROPE_DIM // 2, 1) * s2
        outs.append(y * scale if scale != 1.0 else y)
    return jnp.concatenate(outs, axis=1)


def _silu(x):
    return x * (1.0 / (1.0 + jnp.exp(-x)))


def _proj_kernel(h_ref, w_ref, rope_ref, o_ref, *, groups):
    j = pl.program_id(1)
    for lo, hi, mode in groups:
        @pl.when((j >= lo) & (j < hi))
        def _(mode=mode):
            for r in range(PROJ_TM // PROJ_RB):
                rows = pl.ds(r * PROJ_RB, PROJ_RB)
                acc = jnp.dot(h_ref[rows, :], w_ref[...], preferred_element_type=F32)
                if mode in ("rope_q", "rope_k", "rope_k|plain"):
                    tabs = (rope_ref[0, rows, :], rope_ref[1, rows, :], rope_ref[2, rows, :])
                    if mode == "rope_k|plain":
                        half = acc.shape[1] // 2
                        y = jnp.concatenate([_rope_cols(acc[:, :half], tabs, 1.0), acc[:, half:]], axis=1)
                    else:
                        y = _rope_cols(acc, tabs, Q_SCALE if mode == "rope_q" else 1.0)
                elif mode == "silu":
                    y = _silu(acc)
                else:
                    y = acc
                o_ref[rows, :] = y.astype(o_ref.dtype)


def _proj(h, w, rope_tab, groups, n_out, out_dtype, col_map, name):
    tn = PROJ_TN
    n_j = n_out // tn
    s_blocks = SEQ // PROJ_TM
    return pl.pallas_call(
        functools.partial(_proj_kernel, groups=groups),
        grid=(M_TOK // PROJ_TM, n_j),
        in_specs=[pl.BlockSpec((PROJ_TM, D_MODEL), lambda i, j: (i, 0)),
                  pl.BlockSpec((D_MODEL, tn), lambda i, j: (0, col_map(j))),
                  pl.BlockSpec((3, PROJ_TM, LANES), lambda i, j: (0, i % s_blocks, 0))],
        out_specs=pl.BlockSpec((PROJ_TM, tn), lambda i, j: (i, j)),
        out_shape=jax.ShapeDtypeStruct((M_TOK, n_out), out_dtype),
        compiler_params=_cparams(("parallel", "arbitrary")),
        name=name,
    )(h, w, rope_tab)


def _outproj_kernel(*refs, n_in, final):
    o_refs = refs[:n_in]
    w_ref, x_ref, nw_ref = refs[n_in:n_in + 3]
    outs = refs[n_in + 3:]
    acc = x_ref[...]
    k0 = 0
    for o_ref in o_refs:
        kw = o_ref.shape[1]
        acc = acc + jnp.dot(o_ref[...], w_ref[k0:k0 + kw, :], preferred_element_type=F32)
        k0 += kw
    y = _rms(acc, nw_ref[...])
    if final:
        outs[0][...] = y
    else:
        outs[0][...] = acc
        outs[1][...] = y.astype(BF16)


def _outproj(o_list, w, x, next_norm_w, final, name):
    n_in = len(o_list)
    row = lambda i: (i, 0)
    in_specs = [pl.BlockSpec((OUT_TM, o.shape[1]), row) for o in o_list]
    in_specs += [pl.BlockSpec((D_INNER, D_MODEL), lambda i: (0, 0)),
                 pl.BlockSpec((OUT_TM, D_MODEL), row),
                 pl.BlockSpec((1, D_MODEL), lambda i: (0, 0))]
    if final:
        out_shape = [jax.ShapeDtypeStruct((M_TOK, D_MODEL), F32)]
        out_specs = [pl.BlockSpec((OUT_TM, D_MODEL), row)]
    else:
        out_shape = [jax.ShapeDtypeStruct((M_TOK, D_MODEL), F32),
                     jax.ShapeDtypeStruct((M_TOK, D_MODEL), BF16)]
        out_specs = [pl.BlockSpec((OUT_TM, D_MODEL), row), pl.BlockSpec((OUT_TM, D_MODEL), row)]
    return pl.pallas_call(
        functools.partial(_outproj_kernel, n_in=n_in, final=final),
        grid=(M_TOK // OUT_TM,),
        in_specs=in_specs, out_specs=out_specs, out_shape=out_shape,
        compiler_params=_cparams(("parallel",)),
        name=name,
    )(*o_list, w, x, next_norm_w.reshape(1, D_MODEL))


def _dattn_step(lam_ref, sw_ref, q_ref, k_ref, v_ref, g_ref, o_ref, qm_ref, mx_ref, ls_ref, a_ref, ot_ref, l1_ref,
                s_in, m_in, s_out, m_out, lam_init):
    bounds = np.cumsum((0,) + DA_SUBS)
    assert bounds[-1] == SEQ
    sub8 = (DA_RC // 8, 8, DA_TQ)

    ot = ot_ref[...] / l1_ref[0:1, :]
    yt = ot * lax.rsqrt(jnp.mean(ot * ot, axis=0, keepdims=True) + EPS) * sw_ref[...] * (1.0 - lam_init)
    o_ref[...] = (yt.T * g_ref[...].astype(F32)).astype(o_ref.dtype)

    q = q_ref[...]
    lane = lax.broadcasted_iota(jnp.int32, q.shape, 1)
    qm_ref[0] = jnp.where(lane < HEAD_DIM, q, jnp.zeros_like(q))
    qm_ref[1] = jnp.where(lane >= HEAD_DIM, q, jnp.zeros_like(q))
    mx_ref[...] = jnp.full(mx_ref.shape, NEG_BIG, F32)
    ls_ref[...] = jnp.zeros(ls_ref.shape, F32)

    for lo, hi in zip(bounds[:-1], bounds[1:]):
        lo, hi = int(lo), int(hi)
        kb = k_ref[lo:hi, :]
        for mp in range(2):
            st = lax.dot_general(kb, qm_ref[mp], _NT, preferred_element_type=F32)
            s_in[mp, lo:hi, :] = st
            mx_ref[mp] = jnp.maximum(mx_ref[mp], jnp.max(st.reshape((hi - lo) // 8, 8, DA_TQ), axis=0))
        for mp in range(2):
            m = m_out[mp]
            part = ls_ref[mp]
            for r0 in range(lo, hi, DA_RC):
                rows = slice(r0, r0 + DA_RC)
                p = jnp.exp2(s_out[mp, rows, :].reshape(sub8) - m[None])
                s_out[mp, rows, :] = p.reshape(DA_RC, DA_TQ)
                part = part + jnp.sum(p, axis=0)
            ls_ref[mp] = part

    lp = lam_ref[...]
    lam = (jnp.exp(jnp.sum(lp[0:1] * lp[1:2], axis=1, keepdims=True))
           - jnp.exp(jnp.sum(lp[2:3] * lp[3:4], axis=1, keepdims=True)) + lam_init)
    l1 = jnp.sum(ls_ref[0], axis=0, keepdims=True)
    l2 = jnp.sum(ls_ref[1], axis=0, keepdims=True)
    rho = jnp.broadcast_to(lam * l1 / l2, (8, DA_TQ))
    for r0 in range(0, SEQ, DA_RC):
        rows = slice(r0, r0 + DA_RC)
        a = s_out[0, rows, :].reshape(sub8) - s_out[1, rows, :].reshape(sub8) * rho[None]
        a_ref[rows, :] = a.reshape(DA_RC, DA_TQ).astype(BF16)
    ot_ref[...] = lax.dot_general(v_ref[...], a_ref[...], _TN, preferred_element_type=F32)
    l1_ref[...] = jnp.broadcast_to(l1, (8, DA_TQ))
    for mp in range(2):
        m_in[mp] = jnp.broadcast_to(jnp.max(mx_ref[mp], axis=0, keepdims=True), (8, DA_TQ))


def _dattn_kernel(lam_ref, sw_ref, q_ref, k_ref, v_ref, g_ref, o_ref,
                  qm_ref, s0_ref, s1_ref, m0_ref, m1_ref, mx_ref, ls_ref, a_ref, ot_ref, l1_ref, *, lam_init):
    g = pl.program_id(0)
    common = (lam_ref, sw_ref, q_ref, k_ref, v_ref, g_ref, o_ref, qm_ref, mx_ref, ls_ref, a_ref, ot_ref, l1_ref)

    @pl.when(g == 0)
    def _():
        s1_ref[...] = jnp.zeros(s1_ref.shape, F32)
        m1_ref[...] = jnp.zeros(m1_ref.shape, F32)
        ot_ref[...] = jnp.zeros(ot_ref.shape, F32)
        l1_ref[...] = jnp.ones(l1_ref.shape, F32)

    @pl.when(g % 2 == 0)
    def _():
        _dattn_step(*common, s0_ref, m0_ref, s1_ref, m1_ref, lam_init)

    @pl.when(g % 2 == 1)
    def _():
        _dattn_step(*common, s1_ref, m1_ref, s0_ref, m0_ref, lam_init)


def _dattn(pm, lam_p, subln_w, lam_init, name):
    nq = SEQ // DA_TQ
    hb = DA_WIDTH // LANES
    n_tiles = BATCH * DA_HEADS * nq

    def tile_in(g):
        return jnp.minimum(g, n_tiles - 1)

    def tile_out(g):
        return jnp.clip(g - 1, 0, n_tiles - 1)

    def tile_fin(g):
        return jnp.clip(g - 2, 0, n_tiles - 1)

    def bhi(t):
        return t // (DA_HEADS * nq), (t // nq) % DA_HEADS, t % nq

    def q_map(g):
        b, h, i = bhi(tile_in(g))
        return b * nq + i, h

    def k_map(g):
        b, h, _ = bhi(tile_in(g))
        return b, hb + h

    def v_map(g):
        b, h, _ = bhi(tile_out(g))
        return b, 2 * hb + h

    def g_map(g):
        b, h, i = bhi(tile_fin(g))
        return b * nq + i, 3 * hb + h

    def o_map(g):
        b, h, i = bhi(tile_fin(g))
        return b * nq + i, h

    return pl.pallas_call(
        functools.partial(_dattn_kernel, lam_init=lam_init),
        grid=(n_tiles + 2,),
        in_specs=[pl.BlockSpec((4, HEAD_DIM), lambda g: (0, 0)),
                  pl.BlockSpec((DA_VDIM, 1), lambda g: (0, 0)),
                  pl.BlockSpec((DA_TQ, LANES), q_map),
                  pl.BlockSpec((SEQ, LANES), k_map),
                  pl.BlockSpec((SEQ, LANES), v_map),
                  pl.BlockSpec((DA_TQ, LANES), g_map)],
        out_specs=pl.BlockSpec((DA_TQ, LANES), o_map),
        out_shape=jax.ShapeDtypeStruct((M_TOK, DA_WIDTH), BF16),
        scratch_shapes=[pltpu.VMEM((2, DA_TQ, LANES), BF16),
                        pltpu.VMEM((2, SEQ, DA_TQ), F32),
                        pltpu.VMEM((2, SEQ, DA_TQ), F32),
                        pltpu.VMEM((2, 8, DA_TQ), F32),
                        pltpu.VMEM((2, 8, DA_TQ), F32),
                        pltpu.VMEM((2, 8, DA_TQ), F32),
                        pltpu.VMEM((2, 8, DA_TQ), F32),
                        pltpu.VMEM((SEQ, DA_TQ), BF16),
                        pltpu.VMEM((DA_VDIM, DA_TQ), F32),
                        pltpu.VMEM((8, DA_TQ), F32)],
        compiler_params=_cparams(("arbitrary",)),
        name=name,
    )(lam_p, subln_w.reshape(DA_VDIM, 1), pm, pm, pm, pm)


def _hgrn_gates(z, lb, one_m_lb):
    e = jnp.exp2(jnp.abs(z) * (-LOG2E))
    r = 1.0 / (1.0 + e)
    er = e * r
    pos = z >= 0.0
    f = lb + one_m_lb * jnp.where(pos, r, er)
    log2_f = jnp.where(f > 0.0, jnp.log2(f), z * LOG2E)
    key = one_m_lb * jnp.where(pos, er, r)
    return log2_f, key


def _hgrn_kernel(q_ref, v_ref, g_ref, zf_ref, zb_ref, lb_ref, nw_ref, o_ref,
                 zfp, zbp, qp, lq, lk, ks, a_ref, qgf, qgb, kdf, kdb, oi, ut, st,
                 decf, decb, mask):
    C, G, P = HG_CHUNK, HG_GROUP, HG_PITCH
    n_chunks = SEQ // C
    levels = [2 ** i for i in range(1, C.bit_length())]
    n_lv = len(levels)
    lb = lb_ref[...]
    one_m_lb = 1.0 - lb

    arow = lax.broadcasted_iota(jnp.int32, (C, C), 0)
    acol = lax.broadcasted_iota(jnp.int32, (C, C), 1)
    for li, n in enumerate(levels):
        same = (arow // n) == (acol // n)
        cross = ((arow % n) >= n // 2) != ((acol % n) >= n // 2)
        mask[li] = jnp.where(same & cross, 1.0, 0.0).astype(F32)
    mask[n_lv] = jnp.where(arow == acol, 1.0, 0.0).astype(F32)

    def group_body(grp, _):
        r0 = pl.multiple_of(grp * (G * C), G * C)
        for j in range(G):
            src = pl.ds(r0 + j * C, C)
            dst = slice(j * P, j * P + C)
            zfp[dst, :] = zf_ref[src, :]
            zbp[dst, :] = zb_ref[src, :]
            qp[dst, :] = q_ref[src, :].astype(F32)
        pf, pb, kf, kb, qi = [], [], [], [], []
        for t in range(C):
            sl = pl.ds(t, G, stride=P)
            lf, key_f = _hgrn_gates(zfp[sl, :], lb[0:1], one_m_lb[0:1])
            lbk, key_b = _hgrn_gates(zbp[sl, :], lb[1:2], one_m_lb[1:2])
            pf.append(lf)
            pb.append(lbk)
            kf.append(key_f)
            kb.append(key_b)
            qi.append(qp[sl, :])
            ks[sl, :] = key_f + key_b
        for j in range(G):
            rows = slice(j * P, j * P + C)
            a_ref[j] = mask[n_lv] * jnp.sum(qp[rows, :] * ks[rows, :], axis=1, keepdims=True)
        for li, n in enumerate(levels):
            half = n // 2
            for b0 in range(0, C, n):
                mid = b0 + half
                bf = pf[mid - 1]
                bb = pb[mid]
                for t in range(b0, mid):
                    sl = pl.ds(t, G, stride=P)
                    pbt = pb[t]
                    lk[li, sl, :] = kf[t] * jnp.exp2(bf - pf[t])
                    lq[li, sl, :] = qi[t] * jnp.exp2(pbt)
                    pb[t] = pbt + bb
                for t in range(mid, b0 + n):
                    sl = pl.ds(t, G, stride=P)
                    pft = pf[t]
                    lq[li, sl, :] = qi[t] * jnp.exp2(pft)
                    lk[li, sl, :] = kb[t] * jnp.exp2(bb - pb[t])
                    pf[t] = pft + bf
            for j in range(G):
                rows = slice(j * P, j * P + C)
                an = lax.dot_general(lq[li, rows, :].astype(BF16), lk[li, rows, :].astype(BF16), _NT,
                                     preferred_element_type=F32)
                a_ref[j] += mask[li] * an
        pf_last = pf[C - 1]
        pb_first = pb[0]
        decf[pl.ds(pl.multiple_of(grp * G, G), G), :] = jnp.exp2(pf_last)
        decb[pl.ds(pl.multiple_of(grp * G, G), G), :] = jnp.exp2(pb_first)

        for t in range(C):
            sl = pl.ds(t, G, stride=P)
            pft = pf[t]
            pbt = pb[t]
            qt = qi[t]
            qgf[grp, sl, :] = qt * jnp.exp2(pft)
            qgb[grp, sl, :] = qt * jnp.exp2(pbt)
            kdf[sl, :] = kf[t] * jnp.exp2(pf_last - pft)
            kdb[sl, :] = kb[t] * jnp.exp2(pb_first - pbt)
            if t % (C // G) == 0:
                j = t // (C // G)
                c0 = pl.multiple_of((grp * G + j) * C, C)
                oi[pl.ds(c0, C), :] = jnp.dot(a_ref[j].astype(BF16), v_ref[pl.ds(c0, C), :],
                                              preferred_element_type=F32)
        for j in range(G):
            n = grp * G + j
            rows = slice(j * P, j * P + C)
            kd = jnp.concatenate([kdf[rows, :], kdb[rows, :]], axis=1).astype(BF16)
            ut[n] = lax.dot_general(v_ref[pl.ds(pl.multiple_of(n * C, C), C), :], kd, _TN,
                                    preferred_element_type=F32)
        return 0

    lax.fori_loop(0, n_chunks // G, group_body, 0)

    def fwd_scan(n, cur):
        st[n, :, 0:HG_EXPAND] = cur.astype(BF16)
        return decf[pl.ds(n, 1), :] * cur + ut[n, :, 0:HG_EXPAND]

    lax.fori_loop(0, n_chunks, fwd_scan, jnp.zeros((HG_EXPAND, HG_EXPAND), F32))

    def bwd_scan(i, cur):
        n = n_chunks - 1 - i
        st[n, :, HG_EXPAND:2 * HG_EXPAND] = cur.astype(BF16)
        return decb[pl.ds(n, 1), :] * cur + ut[n, :, HG_EXPAND:2 * HG_EXPAND]

    lax.fori_loop(0, n_chunks, bwd_scan, jnp.zeros((HG_EXPAND, HG_EXPAND), F32))

    def out_group(grp, _):
        for j in range(G):
            n = grp * G + j
            rows = slice(j * P, j * P + C)
            c0 = pl.multiple_of(n * C, C)
            qg = jnp.concatenate([qgf[grp, rows, :], qgb[grp, rows, :]], axis=1).astype(BF16)
            o = oi[pl.ds(c0, C), :] + lax.dot_general(qg, st[n], _NT, preferred_element_type=F32)
            y = _rms(o, nw_ref[...]) * g_ref[pl.ds(c0, C), :].astype(F32)
            o_ref[pl.ds(c0, C), :] = y.astype(o_ref.dtype)
        return 0

    lax.fori_loop(0, n_chunks // G, out_group, 0)


def _hgrn(pm, pg, lb, norm_w, name):
    hb = HG_WIDTH // LANES
    dk = HG_EXPAND
    n_chunks = SEQ // HG_CHUNK
    n_groups = n_chunks // HG_GROUP
    n_lv = HG_CHUNK.bit_length() - 1
    grows = HG_GROUP * HG_PITCH
    seq_blk = lambda off: pl.BlockSpec((SEQ, LANES), lambda b, h: (b, off + h))
    return pl.pallas_call(
        _hgrn_kernel,
        grid=(BATCH, HG_HEADS),
        in_specs=[seq_blk(4 * hb), seq_blk(5 * hb), seq_blk(6 * hb), seq_blk(0), seq_blk(hb),
                  pl.BlockSpec((2, LANES), lambda b, h: (0, h)),
                  pl.BlockSpec((1, HG_EXPAND), lambda b, h: (0, 0))],
        out_specs=pl.BlockSpec((SEQ, LANES), lambda b, h: (b, h)),
        out_shape=jax.ShapeDtypeStruct((M_TOK, HG_WIDTH), BF16),
        scratch_shapes=[
            pltpu.VMEM((grows, dk), F32), pltpu.VMEM((grows, dk), F32), pltpu.VMEM((grows, dk), F32),
            pltpu.VMEM((n_lv, grows, dk), F32), pltpu.VMEM((n_lv, grows, dk), F32),
            pltpu.VMEM((grows, dk), F32),
            pltpu.VMEM((HG_GROUP, HG_CHUNK, HG_CHUNK), F32),
            pltpu.VMEM((n_groups, grows, dk), F32), pltpu.VMEM((n_groups, grows, dk), F32),
            pltpu.VMEM((grows, dk), F32), pltpu.VMEM((grows, dk), F32),
            pltpu.VMEM((SEQ, dk), F32),
            pltpu.VMEM((n_chunks, dk, 2 * dk), F32), pltpu.VMEM((n_chunks, dk, 2 * dk), BF16),
            pltpu.VMEM((n_chunks, dk), F32), pltpu.VMEM((n_chunks, dk), F32),
            pltpu.VMEM((n_lv + 1, HG_CHUNK, HG_CHUNK), F32)],
        compiler_params=_cparams(("parallel", "parallel")),
        name=name,
    )(pm, pm, pm, pg, pg, lb, norm_w.reshape(1, HG_EXPAND))


def _wattn_step(sink_ref, q_ref, kp_ref, kc_ref, kn_ref, vp_ref, vc_ref, vn_ref, g_ref, o_ref,
                bias_ref, p_ref, rden_ref, s_in, s_out, i_in):
    band = 3 * SW_BLOCK
    pairs = SW_GROUP // 2
    kj = lax.broadcasted_iota(jnp.int32, (band, SW_BLOCK), 0) - SW_BLOCK
    qi = lax.broadcasted_iota(jnp.int32, (band, SW_BLOCK), 1)
    kpos = i_in * SW_BLOCK + kj
    valid = (jnp.abs(kj - qi) <= SW_WINDOW) & (kpos >= 0) & (kpos < SEQ)
    bias_ref[...] = jnp.where(valid, 0.0, -jnp.inf).astype(F32)
    low = lax.broadcasted_iota(jnp.int32, (band, LANES), 1) < HEAD_DIM
    for c in range(SW_KV_HEADS):
        cs = slice(c * LANES, (c + 1) * LANES)
        k2 = jnp.concatenate([kp_ref[:, cs], kc_ref[:, cs], kn_ref[:, cs]], axis=0)
        zero = jnp.zeros_like(k2)
        kbd = jnp.concatenate([jnp.where(low, k2, zero), jnp.where(low, zero, k2)], axis=0)
        col0 = c * SW_GROUP * HEAD_DIM
        lhs = jnp.concatenate([q_ref[:, col0 + p * LANES:col0 + (p + 1) * LANES] for p in range(pairs)], axis=0)
        s = lax.dot_general(kbd, lhs, _NT, preferred_element_type=F32)
        for p in range(pairs):
            for hh in range(2):
                s_in[c, hh * band:(hh + 1) * band, p * LANES:(p + 1) * LANES] = (
                    s[hh * band:(hh + 1) * band, p * LANES:(p + 1) * LANES] + bias_ref[...])
        slot = c % 2
        v2 = jnp.concatenate([vp_ref[:, cs], vc_ref[:, cs], vn_ref[:, cs]], axis=0)
        zero = jnp.zeros_like(v2)
        vbd = jnp.concatenate([jnp.where(low, v2, zero), jnp.where(low, zero, v2)], axis=0)
        col0 = c * SW_GROUP * HEAD_DIM
        for p in range(pairs):
            qc = slice(p * LANES, (p + 1) * LANES)
            for hh in range(2):
                sink = sink_ref[c * SW_GROUP + 2 * p + hh] * LOG2E
                rows = slice(hh * band, (hh + 1) * band)
                x = s_out[c, rows, qc]
                mx = jnp.max(x.reshape(band // 8, 8, LANES), axis=0)
                m = jnp.maximum(jnp.max(mx, axis=0, keepdims=True), sink)
                pv = jnp.exp2(x - m)
                p_ref[slot, rows, qc] = pv.astype(BF16)
                den = jnp.sum(pv.reshape(band // 8, 8, LANES), axis=0)
                den = jnp.sum(den, axis=0, keepdims=True) + jnp.exp2(sink - m)
                rden_ref[slot, hh:hh + 1, qc] = 1.0 / den
        ot = lax.dot_general(vbd, p_ref[slot], _TN, preferred_element_type=F32)
        ot = jnp.concatenate([ot[:HEAD_DIM] * rden_ref[slot, 0:1, :], ot[HEAD_DIM:] * rden_ref[slot, 1:2, :]],
                             axis=0)
        for p in range(pairs):
            col = col0 + p * LANES
            o_ref[:, col:col + LANES] = (ot[:, p * LANES:(p + 1) * LANES].T
                                         * g_ref[:, col:col + LANES].astype(F32)).astype(o_ref.dtype)


def _wattn_kernel(sink_ref, q_ref, kp_ref, kc_ref, kn_ref, vp_ref, vc_ref, vn_ref, g_ref, o_ref,
                  bias_ref, s0_ref, s1_ref, p_ref, rden_ref):
    t = pl.program_id(0)
    nb = SEQ // SW_BLOCK
    i_in = jnp.minimum(t, BATCH * nb - 1) % nb
    common = (sink_ref, q_ref, kp_ref, kc_ref, kn_ref, vp_ref, vc_ref, vn_ref, g_ref, o_ref, bias_ref, p_ref,
              rden_ref)

    @pl.when(t == 0)
    def _():
        s1_ref[...] = jnp.zeros(s1_ref.shape, F32)

    @pl.when(t % 2 == 0)
    def _():
        _wattn_step(*common, s0_ref, s1_ref, i_in)

    @pl.when(t % 2 == 1)
    def _():
        _wattn_step(*common, s1_ref, s0_ref, i_in)


def _wattn(po, sink, name):
    nb = SEQ // SW_BLOCK
    n_blocks = BATCH * nb
    kv_w = SW_KV_HEADS * LANES
    kcol = 2 * D_INNER // kv_w
    vcol = kcol + 1
    rows = (SW_GROUP // 2) * SW_BLOCK

    def blk_in(t):
        return jnp.minimum(t, n_blocks - 1)

    def blk_out(t):
        return jnp.maximum(t - 1, 0)

    def kv(col, shift, which):
        def imap(t):
            u = which(t)
            return (u // nb) * nb + jnp.clip(u % nb + shift, 0, nb - 1), col
        return pl.BlockSpec((SW_BLOCK, kv_w), imap)

    return pl.pallas_call(
        _wattn_kernel,
        grid=(n_blocks + 1,),
        in_specs=[pl.BlockSpec(memory_space=pltpu.SMEM),
                  pl.BlockSpec((SW_BLOCK, D_INNER), lambda t: (blk_in(t), 0)),
                  kv(kcol, -1, blk_in), kv(kcol, 0, blk_in), kv(kcol, 1, blk_in),
                  kv(vcol, -1, blk_out), kv(vcol, 0, blk_out), kv(vcol, 1, blk_out),
                  pl.BlockSpec((SW_BLOCK, D_INNER), lambda t: (blk_out(t), 1))],
        out_specs=pl.BlockSpec((SW_BLOCK, D_INNER), lambda t: (blk_out(t), 0)),
        out_shape=jax.ShapeDtypeStruct((M_TOK, D_INNER), BF16),
        scratch_shapes=[pltpu.VMEM((3 * SW_BLOCK, SW_BLOCK), F32),
                        pltpu.VMEM((SW_KV_HEADS, 6 * SW_BLOCK, rows), F32),
                        pltpu.VMEM((SW_KV_HEADS, 6 * SW_BLOCK, rows), F32),
                        pltpu.VMEM((2, 6 * SW_BLOCK, rows), BF16),
                        pltpu.VMEM((2, 8, rows), F32)],
        compiler_params=_cparams(("arbitrary",)),
        name=name,
    )(sink, po, po, po, po, po, po, po, po)


def _rope_tables():
    inv = ROPE_THETA ** (-jnp.arange(0, ROPE_DIM, 2, dtype=F32) / ROPE_DIM)
    ang = jnp.arange(SEQ, dtype=F32)[:, None] * inv[None, :]
    cos, sin = jnp.cos(ang), jnp.sin(ang)
    half = ROPE_DIM // 2
    m = np.arange(LANES) % HEAD_DIM
    idx = jnp.asarray(m % half)
    cos_l, sin_l = cos[:, idx], sin[:, idx]
    c = jnp.where(jnp.asarray(m < ROPE_DIM)[None, :], cos_l, 1.0)
    s1 = jnp.where(jnp.asarray(m < half)[None, :], -sin_l, 0.0)
    s2 = jnp.where(jnp.asarray((m >= half) & (m < ROPE_DIM))[None, :], sin_l, 0.0)
    return jnp.stack([c, s1, s2]).astype(F32)


def _dup_heads(w):
    d, n = w.shape[0], w.shape[1] // HEAD_DIM
    return jnp.repeat(w.reshape(d, n, 1, HEAD_DIM), 2, axis=2).reshape(d, 2 * n * HEAD_DIM)


def kernel(x, norm_even, w_in_even, lambda_qk, subln_w, hgrn_norm_w, hgrn_lb_logits, w_out_even,
           norm_odd, w_in_odd, sink_logits, w_out_odd, final_norm):
    rope_tab = _rope_tables()
    lb_cum = jnp.cumsum(jax.nn.softmax(hgrn_lb_logits.astype(F32), axis=1), axis=1)
    lb_all = lb_cum - lb_cum[:, :1]

    t1 = DA_WIDTH // PROJ_TN
    even_groups = ((0, t1, "rope_q"), (t1, 2 * t1, "rope_k"), (2 * t1, 3 * t1, "plain"),
                   (3 * t1, 5 * t1, "silu"), (5 * t1, 6 * t1, "plain"), (6 * t1, 7 * t1, "silu"))
    n_gate_tiles = 2 * t1
    even_map = lambda j: jnp.where(j < 5 * t1, j, j + n_gate_tiles)
    gate_groups = ((0, n_gate_tiles, "plain"),)
    gate_map = lambda j: j + 5 * t1
    tq = D_INNER // PROJ_TN
    assert 2 * (2 * SW_KV_HEADS * HEAD_DIM) == PROJ_TN
    odd_groups = ((0, tq, "rope_q"), (tq, 2 * tq, "silu"), (2 * tq, 2 * tq + 1, "rope_k|plain"))

    xf = x.reshape(M_TOK, D_MODEL)
    h = _norm(xf, norm_even[0])
    out = None
    for layer in range(DEPTH):
        j = layer // 2
        last = layer == DEPTH - 1
        if last:
            next_norm = final_norm
        elif layer % 2 == 0:
            next_norm = norm_odd[j]
        else:
            next_norm = norm_even[j + 1]
        if layer % 2 == 0:
            lam_init = 0.8 - 0.6 * math.exp(-0.3 * layer)
            w = w_in_even[j].astype(BF16)
            pm = _proj(h, w, rope_tab, even_groups, 7 * DA_WIDTH, BF16, even_map, f"proj_even{j}")
            pg = _proj(h, w, rope_tab, gate_groups, 2 * HG_WIDTH, F32, gate_map, f"proj_gate{j}")
            oa = _dattn(pm, lambda_qk[j], subln_w[j], lam_init, f"diff_attn{j}")
            ob = _hgrn(pm, pg, lb_all[:, j], hgrn_norm_w[j], f"hgrn{j}")
            res = _outproj([oa, ob], w_out_even[j].astype(BF16), xf, next_norm, last, f"outproj_even{j}")
        else:
            wi = w_in_odd[j]
            kv_w = SW_KV_HEADS * HEAD_DIM
            w = jnp.concatenate([wi[:, :D_INNER], wi[:, D_INNER + 2 * kv_w:],
                                 _dup_heads(wi[:, D_INNER:D_INNER + kv_w]),
                                 _dup_heads(wi[:, D_INNER + kv_w:D_INNER + 2 * kv_w])], axis=1).astype(BF16)
            po = _proj(h, w, rope_tab, odd_groups, w.shape[1], BF16, lambda jj: jj, f"proj_odd{j}")
            oc = _wattn(po, sink_logits[j], f"win_attn{j}")
            res = _outproj([oc], w_out_odd[j].astype(BF16), xf, next_norm, last, f"outproj_odd{j}")
        if last:
            out = res[0]
        else:
            xf, h = res
    return out.reshape(BATCH, SEQ, D_MODEL)
```

```python
import functools
import math

import jax
import jax.numpy as jnp
import numpy as np
from jax import lax
from jax.experimental import pallas as pl
from jax.experimental.pallas import tpu as pltpu

D_MODEL = 1024
BATCH = 4
SEQ = 4096
DEPTH = 4
D_INNER = 2 * D_MODEL
HEAD_DIM = 64
ROPE_THETA = 500000.0
ROPE_DIM = HEAD_DIM // 4
EPS = 1e-6
DA_WIDTH = D_INNER // 2
DA_VDIM = 2 * HEAD_DIM
DA_HEADS = DA_WIDTH // DA_VDIM
HG_WIDTH = D_INNER - DA_WIDTH
HG_EXPAND = 128
HG_HEADS = HG_WIDTH // HG_EXPAND
SW_HEADS = D_INNER // HEAD_DIM
SW_KV_HEADS = SW_HEADS // 8
SW_GROUP = SW_HEADS // SW_KV_HEADS
SW_WINDOW = 128
SW_BLOCK = 128
SW_STRIP = 32

M_TOK = BATCH * SEQ
LANES = 128
F32 = jnp.float32
BF16 = jnp.bfloat16
NEG_BIG = -1e30

VMEM_LIMIT = 48 * 1024 * 1024

NORM_TM = 512
PROJ_TM = 1024
PROJ_TN = 1024
PROJ_RB = 256
OUT_TM = 512
DA_TQ = 256
DA_SUBS = (1024, 1024, 1024, 1024)
DA_RC = 128
HG_CHUNK = 128
HG_GROUP = 8
HG_PITCH = HG_CHUNK + 8
LOG2E = math.log2(math.e)

Q_SCALE = HEAD_DIM ** -0.5 * math.log2(math.e)

_NT = (((1,), (1,)), ((), ()))
_TN = (((0,), (0,)), ((), ()))


def _cparams(sem):
    return pltpu.CompilerParams(dimension_semantics=sem, vmem_limit_bytes=VMEM_LIMIT)


def _rms(x, w):
    return x * lax.rsqrt(jnp.mean(x * x, axis=-1, keepdims=True) + EPS) * w


def _norm_kernel(x_ref, w_ref, h_ref):
    h_ref[...] = _rms(x_ref[...], w_ref[...]).astype(h_ref.dtype)


def _norm(x, w):
    return pl.pallas_call(
        _norm_kernel,
        grid=(M_TOK // NORM_TM,),
        in_specs=[pl.BlockSpec((NORM_TM, D_MODEL), lambda i: (i, 0)),
                  pl.BlockSpec((1, D_MODEL), lambda i: (0, 0))],
        out_specs=pl.BlockSpec((NORM_TM, D_MODEL), lambda i: (i, 0)),
        out_shape=jax.ShapeDtypeStruct((M_TOK, D_MODEL), BF16),
        compiler_params=_cparams(("parallel",)),
        name="rmsnorm0",
    )(x, w.reshape(1, D_MODEL))


def _rope_cols(acc, rope_ref, scale):
    c, s1, s2 = rope_ref
    outs = []
    for t in range(acc.shape[1] // LANES):
        xs = acc[:, t * LANES:(t + 1) * LANES]
        y = xs * c + pltpu.roll(xs, LANES - ROPE_DIM // 2, 1) * s1 + pltpu.roll(xs, ROPE_DIM // 2, 1) * s2
        outs.append(y * scale if scale != 1.0 else y)
    return jnp.concatenate(outs, axis=1)


def _silu(x):
    return x * (1.0 / (1.0 + jnp.exp(-x)))


def _proj_kernel(h_ref, w_ref, rope_ref, o_ref, *, groups):
    j = pl.program_id(1)
    for lo, hi, mode in groups:
        @pl.when((j >= lo) & (j < hi))
        def _(mode=mode):
            for r in range(PROJ_TM // PROJ_RB):
                rows = pl.ds(r * PROJ_RB, PROJ_RB)
                acc = jnp.dot(h_ref[rows, :], w_ref[...], preferred_element_type=F32)
                if mode in ("rope_q", "rope_k", "rope_k|plain"):
                    tabs = (rope_ref[0, rows, :], rope_ref[1, rows, :], rope_ref[2, rows, :])
                    if mode == "rope_k|plain":
                        half = acc.shape[1] // 2
                        y = jnp.concatenate([_rope_cols(acc[:, :half], tabs, 1.0), acc[:, half:]], axis=1)
                    else:
                        y = _rope_cols(acc, tabs, Q_SCALE if mode == "rope_q" else 1.0)
                elif mode == "silu":
                    y = _silu(acc)
                else:
                    y = acc
                o_ref[rows, :] = y.astype(o_ref.dtype)


def _proj(h, w, rope_tab, groups, n_out, out_dtype, col_map, name):
    tn = PROJ_TN
    n_j = n_out // tn
    s_blocks = SEQ // PROJ_TM
    return pl.pallas_call(
        functools.partial(_proj_kernel, groups=groups),
        grid=(M_TOK // PROJ_TM, n_j),
        in_specs=[pl.BlockSpec((PROJ_TM, D_MODEL), lambda i, j: (i, 0)),
                  pl.BlockSpec((D_MODEL, tn), lambda i, j: (0, col_map(j))),
                  pl.BlockSpec((3, PROJ_TM, LANES), lambda i, j: (0, i % s_blocks, 0))],
        out_specs=pl.BlockSpec((PROJ_TM, tn), lambda i, j: (i, j)),
        out_shape=jax.ShapeDtypeStruct((M_TOK, n_out), out_dtype),
        compiler_params=_cparams(("parallel", "arbitrary")),
        name=name,
    )(h, w, rope_tab)


def _outproj_kernel(*refs, n_in, final):
    o_refs = refs[:n_in]
    w_ref, x_ref, nw_ref = refs[n_in:n_in + 3]
    outs = refs[n_in + 3:]
    acc = x_ref[...]
    k0 = 0
    for o_ref in o_refs:
        kw = o_ref.shape[1]
        acc = acc + jnp.dot(o_ref[...], w_ref[k0:k0 + kw, :], preferred_element_type=F32)
        k0 += kw
    y = _rms(acc, nw_ref[...])
    if final:
        outs[0][...] = y
    else:
        outs[0][...] = acc
        outs[1][...] = y.astype(BF16)


def _outproj(o_list, w, x, next_norm_w, final, name):
    n_in = len(o_list)
    row = lambda i: (i, 0)
    in_specs = [pl.BlockSpec((OUT_TM, o.shape[1]), row) for o in o_list]
    in_specs += [pl.BlockSpec((D_INNER, D_MODEL), lambda i: (0, 0)),
                 pl.BlockSpec((OUT_TM, D_MODEL), row),
                 pl.BlockSpec((1, D_MODEL), lambda i: (0, 0))]
    if final:
        out_shape = [jax.ShapeDtypeStruct((M_TOK, D_MODEL), F32)]
        out_specs = [pl.BlockSpec((OUT_TM, D_MODEL), row)]
    else:
        out_shape = [jax.ShapeDtypeStruct((M_TOK, D_MODEL), F32),
                     jax.ShapeDtypeStruct((M_TOK, D_MODEL), BF16)]
        out_specs = [pl.BlockSpec((OUT_TM, D_MODEL), row), pl.BlockSpec((OUT_TM, D_MODEL), row)]
    return pl.pallas_call(
        functools.partial(_outproj_kernel, n_in=n_in, final=final),
        grid=(M_TOK // OUT_TM,),
        in_specs=in_specs, out_specs=out_specs, out_shape=out_shape,
        compiler_params=_cparams(("parallel",)),
        name=name,
    )(*o_list, w, x, next_norm_w.reshape(1, D_MODEL))


def _dattn_step(lam_ref, sw_ref, q_ref, k_ref, v_ref, g_ref, o_ref, qm_ref, mx_ref, ls_ref, a_ref, ot_ref, l1_ref,
                s_in, m_in, s_out, m_out, lam_init):
    bounds = np.cumsum((0,) + DA_SUBS)
    assert bounds[-1] == SEQ
    sub8 = (DA_RC // 8, 8, DA_TQ)

    ot = ot_ref[...] / l1_ref[0:1, :]
    yt = ot * lax.rsqrt(jnp.mean(ot * ot, axis=0, keepdims=True) + EPS) * sw_ref[...] * (1.0 - lam_init)
    o_ref[...] = (yt.T * g_ref[...].astype(F32)).astype(o_ref.dtype)

    q = q_ref[...]
    lane = lax.broadcasted_iota(jnp.int32, q.shape, 1)
    qm_ref[0] = jnp.where(lane < HEAD_DIM, q, jnp.zeros_like(q))
    qm_ref[1] = jnp.where(lane >= HEAD_DIM, q, jnp.zeros_like(q))
    mx_ref[...] = jnp.full(mx_ref.shape, NEG_BIG, F32)
    ls_ref[...] = jnp.zeros(ls_ref.shape, F32)

    for lo, hi in zip(bounds[:-1], bounds[1:]):
        lo, hi = int(lo), int(hi)
        kb = k_ref[lo:hi, :]
        for mp in range(2):
            st = lax.dot_general(kb, qm_ref[mp], _NT, preferred_element_type=F32)
            s_in[mp, lo:hi, :] = st
            mx_ref[mp] = jnp.maximum(mx_ref[mp], jnp.max(st.reshape((hi - lo) // 8, 8, DA_TQ), axis=0))
        for mp in range(2):
            m = m_out[mp]
            part = ls_ref[mp]
            for r0 in range(lo, hi, DA_RC):
                rows = slice(r0, r0 + DA_RC)
                p = jnp.exp2(s_out[mp, rows, :].reshape(sub8) - m[None])
                s_out[mp, rows, :] = p.reshape(DA_RC, DA_TQ)
                part = part + jnp.sum(p, axis=0)
            ls_ref[mp] = part

    lp = lam_ref[...]
    lam = (jnp.exp(jnp.sum(lp[0:1] * lp[1:2], axis=1, keepdims=True))
           - jnp.exp(jnp.sum(lp[2:3] * lp[3:4], axis=1, keepdims=True)) + lam_init)
    l1 = jnp.sum(ls_ref[0], axis=0, keepdims=True)
    l2 = jnp.sum(ls_ref[1], axis=0, keepdims=True)
    rho = jnp.broadcast_to(lam * l1 / l2, (8, DA_TQ))
    for r0 in range(0, SEQ, DA_RC):
        rows = slice(r0, r0 + DA_RC)
        a = s_out[0, rows, :].reshape(sub8) - s_out[1, rows, :].reshape(sub8) * rho[None]
        a_ref[rows, :] = a.reshape(DA_RC, DA_TQ).astype(BF16)
    ot_ref[...] = lax.dot_general(v_ref[...], a_ref[...], _TN, preferred_element_type=F32)
    l1_ref[...] = jnp.broadcast_to(l1, (8, DA_TQ))
    for mp in range(2):
        m_in[mp] = jnp.broadcast_to(jnp.max(mx_ref[mp], axis=0, keepdims=True), (8, DA_TQ))


def _dattn_kernel(lam_ref, sw_ref, q_ref, k_ref, v_ref, g_ref, o_ref,
                  qm_ref, s0_ref, s1_ref, m0_ref, m1_ref, mx_ref, ls_ref, a_ref, ot_ref, l1_ref, *, lam_init):
    g = pl.program_id(0)
    common = (lam_ref, sw_ref, q_ref, k_ref, v_ref, g_ref, o_ref, qm_ref, mx_ref, ls_ref, a_ref, ot_ref, l1_ref)

    @pl.when(g == 0)
    def _():
        s1_ref[...] = jnp.zeros(s1_ref.shape, F32)
        m1_ref[...] = jnp.zeros(m1_ref.shape, F32)
        ot_ref[...] = jnp.zeros(ot_ref.shape, F32)
        l1_ref[...] = jnp.ones(l1_ref.shape, F32)

    @pl.when(g % 2 == 0)
    def _():
        _dattn_step(*common, s0_ref, m0_ref, s1_ref, m1_ref, lam_init)

    @pl.when(g % 2 == 1)
    def _():
        _dattn_step(*common, s1_ref, m1_ref, s0_ref, m0_ref, lam_init)


def _dattn(pm, lam_p, subln_w, lam_init, name):
    nq = SEQ // DA_TQ
    hb = DA_WIDTH // LANES
    n_tiles = BATCH * DA_HEADS * nq

    def tile_in(g):
        return jnp.minimum(g, n_tiles - 1)

    def tile_out(g):
        return jnp.clip(g - 1, 0, n_tiles - 1)

    def tile_fin(g):
        return jnp.clip(g - 2, 0, n_tiles - 1)

    def bhi(t):
        return t // (DA_HEADS * nq), (t // nq) % DA_HEADS, t % nq

    def q_map(g):
        b, h, i = bhi(tile_in(g))
        return b * nq + i, h

    def k_map(g):
        b, h, _ = bhi(tile_in(g))
        return b, hb + h

    def v_map(g):
        b, h, _ = bhi(tile_out(g))
        return b, 2 * hb + h

    def g_map(g):
        b, h, i = bhi(tile_fin(g))
        return b * nq + i, 3 * hb + h

    def o_map(g):
        b, h, i = bhi(tile_fin(g))
        return b * nq + i, h

    return pl.pallas_call(
        functools.partial(_dattn_kernel, lam_init=lam_init),
        grid=(n_tiles + 2,),
        in_specs=[pl.BlockSpec((4, HEAD_DIM), lambda g: (0, 0)),
                  pl.BlockSpec((DA_VDIM, 1), lambda g: (0, 0)),
                  pl.BlockSpec((DA_TQ, LANES), q_map),
                  pl.BlockSpec((SEQ, LANES), k_map),
                  pl.BlockSpec((SEQ, LANES), v_map),
                  pl.BlockSpec((DA_TQ, LANES), g_map)],
        out_specs=pl.BlockSpec((DA_TQ, LANES), o_map),
        out_shape=jax.ShapeDtypeStruct((M_TOK, DA_WIDTH), BF16),
        scratch_shapes=[pltpu.VMEM((2, DA_TQ, LANES), BF16),
                        pltpu.VMEM((2, SEQ, DA_TQ), F32),
                        pltpu.VMEM((2, SEQ, DA_TQ), F32),
                        pltpu.VMEM((2, 8, DA_TQ), F32),
                        pltpu.VMEM((2, 8, DA_TQ), F32),
                        pltpu.VMEM((2, 8, DA_TQ), F32),
                        pltpu.VMEM((2, 8, DA_TQ), F32),
                        pltpu.VMEM((SEQ, DA_TQ), BF16),
                        pltpu.VMEM((DA_VDIM, DA_TQ), F32),
                        pltpu.VMEM((8, DA_TQ), F32)],
        compiler_params=_cparams(("arbitrary",)),
        name=name,
    )(lam_p, subln_w.reshape(DA_VDIM, 1), pm, pm, pm, pm)


def _hgrn_gates(z, lb, one_m_lb):
    e = jnp.exp2(jnp.abs(z) * (-LOG2E))
    r = 1.0 / (1.0 + e)
    er = e * r
    pos = z >= 0.0
    f = lb + one_m_lb * jnp.where(pos, r, er)
    log2_f = jnp.where(f > 0.0, jnp.log2(f), z * LOG2E)
    key = one_m_lb * jnp.where(pos, er, r)
    return log2_f, key


def _hgrn_kernel(q_ref, v_ref, g_ref, zf_ref, zb_ref, lb_ref, nw_ref, o_ref,
                 zfp, zbp, qp, lq, lk, ks, a_ref, qgf, qgb, kdf, kdb, oi, ut, st,
                 decf, decb, mask):
    C, G, P = HG_CHUNK, HG_GROUP, HG_PITCH
    n_chunks = SEQ // C
    levels = [2 ** i for i in range(1, C.bit_length())]
    n_lv = len(levels)
    lb = lb_ref[...]
    one_m_lb = 1.0 - lb

    arow = lax.broadcasted_iota(jnp.int32, (C, C), 0)
    acol = lax.broadcasted_iota(jnp.int32, (C, C), 1)
    for li, n in enumerate(levels):
        same = (arow // n) == (acol // n)
        cross = ((arow % n) >= n // 2) != ((acol % n) >= n // 2)
        mask[li] = jnp.where(same & cross, 1.0, 0.0).astype(F32)
    mask[n_lv] = jnp.where(arow == acol, 1.0, 0.0).astype(F32)

    def group_body(grp, _):
        r0 = pl.multiple_of(grp * (G * C), G * C)
        for j in range(G):
            src = pl.ds(r0 + j * C, C)
            dst = slice(j * P, j * P + C)
            zfp[dst, :] = zf_ref[src, :]
            zbp[dst, :] = zb_ref[src, :]
            qp[dst, :] = q_ref[src, :].astype(F32)
        pf, pb, kf, kb, qi = [], [], [], [], []
        for t in range(C):
            sl = pl.ds(t, G, stride=P)
            lf, key_f = _hgrn_gates(zfp[sl, :], lb[0:1], one_m_lb[0:1])
            lbk, key_b = _hgrn_gates(zbp[sl, :], lb[1:2], one_m_lb[1:2])
            pf.append(lf)
            pb.append(lbk)
            kf.append(key_f)
            kb.append(key_b)
            qi.append(qp[sl, :])
            ks[sl, :] = key_f + key_b
        for j in range(G):
            rows = slice(j * P, j * P + C)
            a_ref[j] = mask[n_lv] * jnp.sum(qp[rows, :] * ks[rows, :], axis=1, keepdims=True)
        for li, n in enumerate(levels):
            half = n // 2
            for b0 in range(0, C, n):
                mid = b0 + half
                bf = pf[mid - 1]
                bb = pb[mid]
                for t in range(b0, mid):
                    sl = pl.ds(t, G, stride=P)
                    pbt = pb[t]
                    lk[li, sl, :] = kf[t] * jnp.exp2(bf - pf[t])
                    lq[li, sl, :] = qi[t] * jnp.exp2(pbt)
                    pb[t] = pbt + bb
                for t in range(mid, b0 + n):
                    sl = pl.ds(t, G, stride=P)
                    pft = pf[t]
                    lq[li, sl, :] = qi[t] * jnp.exp2(pft)
                    lk[li, sl, :] = kb[t] * jnp.exp2(bb - pb[t])
                    pf[t] = pft + bf
            for j in range(G):
                rows = slice(j * P, j * P + C)
                an = lax.dot_general(lq[li, rows, :].astype(BF16), lk[li, rows, :].astype(BF16), _NT,
                                     preferred_element_type=F32)
                a_ref[j] += mask[li] * an
        pf_last = pf[C - 1]
        pb_first = pb[0]
        decf[pl.ds(pl.multiple_of(grp * G, G), G), :] = jnp.exp2(pf_last)
        decb[pl.ds(pl.multiple_of(grp * G, G), G), :] = jnp.exp2(pb_first)

        for t in range(C):
            sl = pl.ds(t, G, stride=P)
            pft = pf[t]
            pbt = pb[t]
            qt = qi[t]
            qgf[grp, sl, :] = qt * jnp.exp2(pft)
            qgb[grp, sl, :] = qt * jnp.exp2(pbt)
            kdf[sl, :] = kf[t] * jnp.exp2(pf_last - pft)
            kdb[sl, :] = kb[t] * jnp.exp2(pb_first - pbt)
            if t % (C // G) == 0:
                j = t // (C // G)
                c0 = pl.multiple_of((grp * G + j) * C, C)
                oi[pl.ds(c0, C), :] = jnp.dot(a_ref[j].astype(BF16), v_ref[pl.ds(c0, C), :],
                                              preferred_element_type=F32)
        for j in range(G):
            n = grp * G + j
            rows = slice(j * P, j * P + C)
            kd = jnp.concatenate([kdf[rows, :], kdb[rows, :]], axis=1).astype(BF16)
            ut[n] = lax.dot_general(v_ref[pl.ds(pl.multiple_of(n * C, C), C), :], kd, _TN,
                                    preferred_element_type=F32)
        return 0

    lax.fori_loop(0, n_chunks // G, group_body, 0)

    def fwd_scan(n, cur):
        st[n, :, 0:HG_EXPAND] = cur.astype(BF16)
        return decf[pl.ds(n, 1), :] * cur + ut[n, :, 0:HG_EXPAND]

    lax.fori_loop(0, n_chunks, fwd_scan, jnp.zeros((HG_EXPAND, HG_EXPAND), F32))

    def bwd_scan(i, cur):
        n = n_chunks - 1 - i
        st[n, :, HG_EXPAND:2 * HG_EXPAND] = cur.astype(BF16)
        return decb[pl.ds(n, 1), :] * cur + ut[n, :, HG_EXPAND:2 * HG_EXPAND]

    lax.fori_loop(0, n_chunks, bwd_scan, jnp.zeros((HG_EXPAND, HG_EXPAND), F32))

    def out_group(grp, _):
        for j in range(G):
            n = grp * G + j
            rows = slice(j * P, j * P + C)
            c0 = pl.multiple_of(n * C, C)
            qg = jnp.concatenate([qgf[grp, rows, :], qgb[grp, rows, :]], axis=1).astype(BF16)
            o = oi[pl.ds(c0, C), :] + lax.dot_general(qg, st[n], _NT, preferred_element_type=F32)
            y = _rms(o, nw_ref[...]) * g_ref[pl.ds(c0, C), :].astype(F32)
            o_ref[pl.ds(c0, C), :] = y.astype(o_ref.dtype)
        return 0

    lax.fori_loop(0, n_chunks // G, out_group, 0)


def _hgrn(pm, pg, lb, norm_w, name):
    hb = HG_WIDTH // LANES
    dk = HG_EXPAND
    n_chunks = SEQ // HG_CHUNK
    n_groups = n_chunks // HG_GROUP
    n_lv = HG_CHUNK.bit_length() - 1
    grows = HG_GROUP * HG_PITCH
    seq_blk = lambda off: pl.BlockSpec((SEQ, LANES), lambda b, h: (b, off + h))
    return pl.pallas_call(
        _hgrn_kernel,
        grid=(BATCH, HG_HEADS),
        in_specs=[seq_blk(4 * hb), seq_blk(5 * hb), seq_blk(6 * hb), seq_blk(0), seq_blk(hb),
                  pl.BlockSpec((2, LANES), lambda b, h: (0, h)),
                  pl.BlockSpec((1, HG_EXPAND), lambda b, h: (0, 0))],
        out_specs=pl.BlockSpec((SEQ, LANES), lambda b, h: (b, h)),
        out_shape=jax.ShapeDtypeStruct((M_TOK, HG_WIDTH), BF16),
        scratch_shapes=[
            pltpu.VMEM((grows, dk), F32), pltpu.VMEM((grows, dk), F32), pltpu.VMEM((grows, dk), F32),
            pltpu.VMEM((n_lv, grows, dk), F32), pltpu.VMEM((n_lv, grows, dk), F32),
            pltpu.VMEM((grows, dk), F32),
            pltpu.VMEM((HG_GROUP, HG_CHUNK, HG_CHUNK), F32),
            pltpu.VMEM((n_groups, grows, dk), F32), pltpu.VMEM((n_groups, grows, dk), F32),
            pltpu.VMEM((grows, dk), F32), pltpu.VMEM((grows, dk), F32),
            pltpu.VMEM((SEQ, dk), F32),
            pltpu.VMEM((n_chunks, dk, 2 * dk), F32), pltpu.VMEM((n_chunks, dk, 2 * dk), BF16),
            pltpu.VMEM((n_chunks, dk), F32), pltpu.VMEM((n_chunks, dk), F32),
            pltpu.VMEM((n_lv + 1, HG_CHUNK, HG_CHUNK), F32)],
        compiler_params=_cparams(("parallel", "parallel")),
        name=name,
    )(pm, pm, pm, pg, pg, lb, norm_w.reshape(1, HG_EXPAND))


def _wattn_step(sink_ref, q_ref, kc_ref, kn_ref, vc_ref, vn_ref, g_ref, o_ref,
                bias_ref, p_ref, rden_ref, kp_ref, vp_ref, s_in, s_out, i_in):
    band = 3 * SW_BLOCK
    pairs = SW_GROUP // 2
    kj = lax.broadcasted_iota(jnp.int32, (band, SW_BLOCK), 0) - SW_BLOCK
    qi = lax.broadcasted_iota(jnp.int32, (band, SW_BLOCK), 1)
    kpos = i_in * SW_BLOCK + kj
    valid = (jnp.abs(kj - qi) <= SW_WINDOW) & (kpos >= 0) & (kpos < SEQ)
    bias_ref[...] = jnp.where(valid, 0.0, -jnp.inf).astype(F32)
    low = lax.broadcasted_iota(jnp.int32, (band, LANES), 1) < HEAD_DIM
    for c in range(SW_KV_HEADS):
        cs = slice(c * LANES, (c + 1) * LANES)
        k2 = jnp.concatenate([kp_ref[:, cs], kc_ref[:, cs], kn_ref[:, cs]], axis=0)
        zero = jnp.zeros_like(k2)
        kbd = jnp.concatenate([jnp.where(low, k2, zero), jnp.where(low, zero, k2)], axis=0)
        col0 = c * SW_GROUP * HEAD_DIM
        lhs = jnp.concatenate([q_ref[:, col0 + p * LANES:col0 + (p + 1) * LANES] for p in range(pairs)], axis=0)
        s = lax.dot_general(kbd, lhs, _NT, preferred_element_type=F32)
        for p in range(pairs):
            for hh in range(2):
                s_in[c, hh * band:(hh + 1) * band, p * LANES:(p + 1) * LANES] = (
                    s[hh * band:(hh + 1) * band, p * LANES:(p + 1) * LANES] + bias_ref[...])
        slot = c % 2
        v2 = jnp.concatenate([vp_ref[:, cs], vc_ref[:, cs], vn_ref[:, cs]], axis=0)
        zero = jnp.zeros_like(v2)
        vbd = jnp.concatenate([jnp.where(low, v2, zero), jnp.where(low, zero, v2)], axis=0)
        for p in range(pairs):
            qc = slice(p * LANES, (p + 1) * LANES)
            for hh in range(2):
                sink = sink_ref[c * SW_GROUP + 2 * p + hh] * LOG2E
                rows = slice(hh * band, (hh + 1) * band)
                x = s_out[c, rows, qc]
                mx = jnp.max(x.reshape(band // 8, 8, LANES), axis=0)
                m = jnp.maximum(jnp.max(mx, axis=0, keepdims=True), sink)
                pv = jnp.exp2(x - m)
                p_ref[slot, rows, qc] = pv.astype(BF16)
                den = jnp.sum(pv.reshape(band // 8, 8, LANES), axis=0)
                den = jnp.sum(den, axis=0, keepdims=True) + jnp.exp2(sink - m)
                rden_ref[slot, hh:hh + 1, qc] = 1.0 / den
        ot = lax.dot_general(vbd, p_ref[slot], _TN, preferred_element_type=F32)
        ot = jnp.concatenate([ot[:HEAD_DIM] * rden_ref[slot, 0:1, :], ot[HEAD_DIM:] * rden_ref[slot, 1:2, :]],
                             axis=0)
        for p in range(pairs):
            col = col0 + p * LANES
            o_ref[:, col:col + LANES] = (ot[:, p * LANES:(p + 1) * LANES].T
                                         * g_ref[:, col:col + LANES].astype(F32)).astype(o_ref.dtype)
    kp_ref[...] = kc_ref[...]
    vp_ref[...] = vc_ref[...]


def _wattn_kernel(sink_ref, q_ref, kc_ref, kn_ref, vc_ref, vn_ref, g_ref, o_ref,
                  bias_ref, s0_ref, s1_ref, p_ref, rden_ref, kp_ref, vp_ref):
    t = pl.program_id(0)
    nb = SEQ // SW_BLOCK
    i_in = jnp.minimum(t, BATCH * nb - 1) % nb
    common = (sink_ref, q_ref, kc_ref, kn_ref, vc_ref, vn_ref, g_ref, o_ref, bias_ref, p_ref, rden_ref,
              kp_ref, vp_ref)

    @pl.when(t == 0)
    def _():
        s1_ref[...] = jnp.zeros(s1_ref.shape, F32)
        kp_ref[...] = jnp.zeros(kp_ref.shape, BF16)
        vp_ref[...] = jnp.zeros(vp_ref.shape, BF16)

    @pl.when(t % 2 == 0)
    def _():
        _wattn_step(*common, s0_ref, s1_ref, i_in)

    @pl.when(t % 2 == 1)
    def _():
        _wattn_step(*common, s1_ref, s0_ref, i_in)


def _wattn(po, sink, name):
    nb = SEQ // SW_BLOCK
    n_blocks = BATCH * nb
    kv_w = SW_KV_HEADS * LANES
    kcol = 2 * D_INNER // kv_w
    vcol = kcol + 1
    rows = (SW_GROUP // 2) * SW_BLOCK

    def blk_in(t):
        return jnp.minimum(t, n_blocks - 1)

    def blk_out(t):
        return jnp.maximum(t - 1, 0)

    def kv(col, shift, which):
        def imap(t):
            u = which(t)
            return (u // nb) * nb + jnp.clip(u % nb + shift, 0, nb - 1), col
        return pl.BlockSpec((SW_BLOCK, kv_w), imap)

    return pl.pallas_call(
        _wattn_kernel,
        grid=(n_blocks + 1,),
        in_specs=[pl.BlockSpec(memory_space=pltpu.SMEM),
                  pl.BlockSpec((SW_BLOCK, D_INNER), lambda t: (blk_in(t), 0)),
                  kv(kcol, 0, blk_in), kv(kcol, 1, blk_in),
                  kv(vcol, 0, blk_out), kv(vcol, 1, blk_out),
                  pl.BlockSpec((SW_BLOCK, D_INNER), lambda t: (blk_out(t), 1))],
        out_specs=pl.BlockSpec((SW_BLOCK, D_INNER), lambda t: (blk_out(t), 0)),
        out_shape=jax.ShapeDtypeStruct((M_TOK, D_INNER), BF16),
        scratch_shapes=[pltpu.VMEM((3 * SW_BLOCK, SW_BLOCK), F32),
                        pltpu.VMEM((SW_KV_HEADS, 6 * SW_BLOCK, rows), F32),
                        pltpu.VMEM((SW_KV_HEADS, 6 * SW_BLOCK, rows), F32),
                        pltpu.VMEM((2, 6 * SW_BLOCK, rows), BF16),
                        pltpu.VMEM((2, 8, rows), F32),
                        pltpu.VMEM((SW_BLOCK, kv_w), BF16),
                        pltpu.VMEM((SW_BLOCK, kv_w), BF16)],
        compiler_params=_cparams(("arbitrary",)),
        name=name,
    )(sink, po, po, po, po, po, po)


def _rope_tables():
    inv = ROPE_THETA ** (-jnp.arange(0, ROPE_DIM, 2, dtype=F32) / ROPE_DIM)
    ang = jnp.arange(SEQ, dtype=F32)[:, None] * inv[None, :]
    cos, sin = jnp.cos(ang), jnp.sin(ang)
    half = ROPE_DIM // 2
    m = np.arange(LANES) % HEAD_DIM
    idx = jnp.asarray(m % half)
    cos_l, sin_l = cos[:, idx], sin[:, idx]
    c = jnp.where(jnp.asarray(m < ROPE_DIM)[None, :], cos_l, 1.0)
    s1 = jnp.where(jnp.asarray(m < half)[None, :], -sin_l, 0.0)
    s2 = jnp.where(jnp.asarray((m >= half) & (m < ROPE_DIM))[None, :], sin_l, 0.0)
    return jnp.stack([c, s1, s2]).astype(F32)


def _dup_heads(w):
    d, n = w.shape[0], w.shape[1] // HEAD_DIM
    return jnp.repeat(w.reshape(d, n, 1, HEAD_DIM), 2, axis=2).reshape(d, 2 * n * HEAD_DIM)


def kernel(x, norm_even, w_in_even, lambda_qk, subln_w, hgrn_norm_w, hgrn_lb_logits, w_out_even,
           norm_odd, w_in_odd, sink_logits, w_out_odd, final_norm):
    rope_tab = _rope_tables()
    lb_cum = jnp.cumsum(jax.nn.softmax(hgrn_lb_logits.astype(F32), axis=1), axis=1)
    lb_all = lb_cum - lb_cum[:, :1]

    t1 = DA_WIDTH // PROJ_TN
    even_groups = ((0, t1, "rope_q"), (t1, 2 * t1, "rope_k"), (2 * t1, 3 * t1, "plain"),
                   (3 * t1, 5 * t1, "silu"), (5 * t1, 6 * t1, "plain"), (6 * t1, 7 * t1, "silu"))
    n_gate_tiles = 2 * t1
    even_map = lambda j: jnp.where(j < 5 * t1, j, j + n_gate_tiles)
    gate_groups = ((0, n_gate_tiles, "plain"),)
    gate_map = lambda j: j + 5 * t1
    tq = D_INNER // PROJ_TN
    assert 2 * (2 * SW_KV_HEADS * HEAD_DIM) == PROJ_TN
    odd_groups = ((0, tq, "rope_q"), (tq, 2 * tq, "silu"), (2 * tq, 2 * tq + 1, "rope_k|plain"))

    xf = x.reshape(M_TOK, D_MODEL)
    h = _norm(xf, norm_even[0])
    out = None
    for layer in range(DEPTH):
        j = layer // 2
        last = layer == DEPTH - 1
        if last:
            next_norm = final_norm
        elif layer % 2 == 0:
            next_norm = norm_odd[j]
        else:
            next_norm = norm_even[j + 1]
        if layer % 2 == 0:
            lam_init = 0.8 - 0.6 * math.exp(-0.3 * layer)
            w = w_in_even[j].astype(BF16)
            pm = _proj(h, w, rope_tab, even_groups, 7 * DA_WIDTH, BF16, even_map, f"proj_even{j}")
            pg = _proj(h, w, rope_tab, gate_groups, 2 * HG_WIDTH, F32, gate_map, f"proj_gate{j}")
            oa = _dattn(pm, lambda_qk[j], subln_w[j], lam_init, f"diff_attn{j}")
            ob = _hgrn(pm, pg, lb_all[:, j], hgrn_norm_w[j], f"hgrn{j}")
            res = _outproj([oa, ob], w_out_even[j].astype(BF16), xf, next_norm, last, f"outproj_even{j}")
        else:
            wi = w_in_odd[j]
            kv_w = SW_KV_HEADS * HEAD_DIM
            w = jnp.concatenate([wi[:, :D_INNER], wi[:, D_INNER + 2 * kv_w:],
                                 _dup_heads(wi[:, D_INNER:D_INNER + kv_w]),
                                 _dup_heads(wi[:, D_INNER + kv_w:D_INNER + 2 * kv_w])], axis=1).astype(BF16)
            po = _proj(h, w, rope_tab, odd_groups, w.shape[1], BF16, lambda jj: jj, f"proj_odd{j}")
            oc = _wattn(po, sink_logits[j], f"win_attn{j}")
            res = _outproj([oc], w_out_odd[j].astype(BF16), xf, next_norm, last, f"outproj_odd{j}")
        if last:
            out = res[0]
        else:
            xf, h = res
    return out.reshape(BATCH, SEQ, D_MODEL)
```
